```python
import numpy as np
import jax, jax.numpy as jnp
from jax import lax

D_MODEL = 1024
BATCH = 2
SEQ = 8192
DEPTH = 2

GRID_W = 64
CTX_LEN = 256
M_HEADS = 4
M_DQK = 128
M_DV = 256
M_CHUNK = 64
M_QK = M_HEADS * M_DQK
M_V = M_HEADS * M_DV
M_GATES = 2 * 2 * M_HEADS
A_HEADS = 8
A_KV_HEADS = 2
A_GROUP = A_HEADS // A_KV_HEADS
A_HD = 128
A_Q = A_HEADS * A_HD
A_KV = A_KV_HEADS * A_HD
Q_BLOCK = 128
ROPE_THETA = 10000.0
C_WIDTH = D_MODEL
CONV_K = 3
D_FF = 4 * D_MODEL
N_MOD = 6
EPS = 1e-6
COL_SIZES = (M_QK, M_QK, M_V, M_V, M_GATES, A_Q, A_KV, A_KV, C_WIDTH, C_WIDTH, C_WIDTH, D_MODEL, D_MODEL, D_MODEL)
IN_COLS = sum(COL_SIZES)

kernel_name = 'hybrid_mlstm_gqa_shortconv_prefix_block'


def rms_norm(x, g):
    xf = x.astype(jnp.float32)
    y = xf * lax.rsqrt(jnp.mean(xf * xf, axis=-1, keepdims=True) + EPS)
    return (y * g).astype(x.dtype)


def modulate(x, shift, scale):
    return x * (1 + scale) + shift


def heads(t, n):
    return t.reshape(*t.shape[:-1], n, t.shape[-1] // n)


def split_columns(z):
    return jnp.split(z, np.cumsum(COL_SIZES)[:-1].tolist(), axis=-1)


def axial_rope_tables(n_tokens):
    rows = n_tokens // GRID_W
    row_idx = jnp.repeat(jnp.arange(rows), GRID_W)
    col_idx = jnp.tile(jnp.arange(GRID_W), rows)
    n_freq = A_HD // 4
    inv_freq = ROPE_THETA ** (-jnp.arange(n_freq, dtype=jnp.float32) / n_freq)
    pos = jnp.stack([row_idx, col_idx], axis=-1).astype(jnp.float32)
    ang = pos[:, :, None] * inv_freq
    return jnp.cos(ang), jnp.sin(ang)


def apply_rope(x, cos, sin):
    b, s, h, d = x.shape
    xr = x.astype(jnp.float32).reshape(b, s, h, 2, 2, d // 4)
    x1, x2 = xr[..., 0, :], xr[..., 1, :]
    cs = cos[None, :, None]
    sn = sin[None, :, None]
    out = jnp.stack([x1 * cs - x2 * sn, x1 * sn + x2 * cs], axis=-2)
    return out.reshape(b, s, h, d).astype(x.dtype)


def mlstm_scan(q, k, v, log_i, log_f, state):
    b, h, s, _ = q.shape
    dv = v.shape[-1]
    nc = s // M_CHUNK

    def chunks(t):
        return jnp.moveaxis(t.reshape(b, h, nc, M_CHUNK, *t.shape[3:]), 2, 0)

    tri = jnp.tril(jnp.ones((M_CHUNK, M_CHUNK), dtype=bool))

    def step(carry, inp):
        c_mat, n_vec, m = carry
        qc, kc, vc, ic, fc = inp
        bcum = jnp.cumsum(fc, axis=-1)
        d = jnp.where(tri, bcum[..., :, None] - bcum[..., None, :] + ic[..., None, :], -jnp.inf)
        inter = bcum + m[..., None]
        m_t = jnp.maximum(inter, jnp.max(d, axis=-1))
        w_intra = jnp.exp(d - m_t[..., None])
        w_inter = jnp.exp(inter - m_t)
        s_qk = jnp.einsum('bhtd,bhsd->bhts', qc, kc) * w_intra
        num = jnp.einsum('bhts,bhsv->bhtv', s_qk, vc) + w_inter[..., None] * jnp.einsum('bhtd,bhdv->bhtv', qc, c_mat)
        den = jnp.sum(s_qk, axis=-1) + w_inter * jnp.einsum('bhtd,bhd->bht', qc, n_vec)
        h_out = num / jnp.maximum(jnp.abs(den), jnp.exp(-m_t))[..., None]
        b_last = bcum[..., -1]
        g = b_last[..., None] - bcum + ic
        m_new = jnp.maximum(b_last + m, jnp.max(g, axis=-1))
        w_g = jnp.exp(g - m_new[..., None])
        w_c = jnp.exp(b_last + m - m_new)
        c_new = w_c[..., None, None] * c_mat + jnp.einsum('bhs,bhsd,bhsv->bhdv', w_g, kc, vc)
        n_new = w_c[..., None] * n_vec + jnp.einsum('bhs,bhsd->bhd', w_g, kc)
        return (c_new, n_new, m_new), h_out

    state, hs = lax.scan(step, state, (chunks(q), chunks(k), chunks(v), chunks(log_i), chunks(log_f)))
    return state, jnp.moveaxis(hs, 0, 2).reshape(b, h, s, dv)


def mlstm_inputs(parts, gate_b):
    mq, mk, mv, mg = parts[0], parts[1], parts[2], parts[4]
    bsz, n = mq.shape[:2]

    def hm(t):
        return jnp.swapaxes(heads(t, M_HEADS), 1, 2).astype(jnp.float32)

    gates = (mg.reshape(bsz, n, 2, 2, M_HEADS).astype(jnp.float32) + gate_b).transpose(0, 2, 3, 4, 1)
    return hm(mq) * (M_DQK ** -0.5), hm(mk), hm(mv), gates


def mlstm_bidirectional(q_l, k_l, v_l, g_l, q_c, k_c, v_c, g_c):
    bsz = q_l.shape[0]
    init = (jnp.zeros((bsz, M_HEADS, M_DQK, M_DV), jnp.float32),
            jnp.zeros((bsz, M_HEADS, M_DQK), jnp.float32),
            jnp.zeros((bsz, M_HEADS), jnp.float32))
    outs = []
    for direction in range(2):
        rev = (lambda t: jnp.flip(t, axis=2)) if direction == 1 else (lambda t: t)

        def run(q, k, v, g, state):
            return mlstm_scan(rev(q), rev(k), rev(v), rev(g[:, direction, 0]),
                              rev(jax.nn.log_sigmoid(g[:, direction, 1])), state)

        ctx_state, h_c = run(q_c, k_c, v_c, g_c, init)
        _, h_l = run(q_l, k_l, v_l, g_l, ctx_state)
        outs.append((rev(h_l), rev(h_c)))
    return outs[0][0] + outs[1][0], outs[0][1] + outs[1][1]


def mlstm_output(h, o_pre, gain):
    h = jnp.swapaxes(h, 1, 2)
    h = h * lax.rsqrt(jnp.mean(h * h, axis=-1, keepdims=True) + EPS)
    h = h.reshape(*h.shape[:2], M_V) * gain
    return (h * jax.nn.sigmoid(o_pre.astype(jnp.float32))).astype(o_pre.dtype)


def gqa_attend(q, k, v):
    s = jnp.einsum('bqhgd,bkhd->bhgqk', q, k).astype(jnp.float32) * (A_HD ** -0.5)
    p = jax.nn.softmax(s, axis=-1).astype(v.dtype)
    return jnp.einsum('bhgqk,bkhd->bqhgd', p, v)


def latent_attention(q, k_all, v_all):
    b, s = q.shape[:2]
    nb = s // Q_BLOCK
    qb = jnp.moveaxis(q.reshape(b, nb, Q_BLOCK, A_KV_HEADS, A_GROUP, A_HD), 1, 0)
    out = lax.map(lambda blk: gqa_attend(blk, k_all, v_all), qb)
    return jnp.moveaxis(out, 0, 1).reshape(b, s, A_Q)


def context_attention(q, k, v):
    b, n = q.shape[:2]
    return gqa_attend(q.reshape(b, n, A_KV_HEADS, A_GROUP, A_HD), k, v).reshape(b, n, A_Q)


def attn_q(parts, q_g):
    return rms_norm(heads(parts[5], A_HEADS), q_g)


def attn_kv(parts, k_g):
    return rms_norm(heads(parts[6], A_KV_HEADS), k_g), heads(parts[7], A_KV_HEADS)


def short_conv(u, w):
    up = jnp.pad(u, ((0, 0), (1, 1), (0, 0)))
    return w[0] * up[:, :-2] + w[1] * up[:, 1:-1] + w[2] * up[:, 2:]


def conv_branch(parts, w):
    u, gate_b, gate_c = parts[8], parts[9], parts[10]
    return gate_b * short_conv(gate_c * u, w)


def merge_branches(y_m, y_a, y_c, parts, w_pm, w_pa, w_pc, w_o):
    ga, gb, gc = parts[11], parts[12], parts[13]
    merged = (jax.nn.sigmoid(ga) * (y_m @ w_pm) + jax.nn.sigmoid(gb) * (y_a @ w_pa)
              + jax.nn.sigmoid(gc) * (y_c @ w_pc))
    return merged @ w_o


def squared_relu_mlp(h, w_up, w_down):
    return jnp.square(jax.nn.relu(h @ w_up)) @ w_down


def setup_inputs(seed: int = 0) -> dict:
    key = jax.random.key(seed)
    ks = jax.random.split(key, 24)
    nrm = jax.random.normal
    f32 = jnp.float32
    ib = 0.1 * nrm(ks[8], (DEPTH, 2, 1, M_HEADS), f32)
    fb = 3.0 + jnp.linspace(0.0, 3.0, M_HEADS, dtype=f32)[None, None, None, :] + 0.1 * nrm(ks[9], (DEPTH, 2, 1, M_HEADS), f32)
    return {
        'x': nrm(ks[0], (BATCH, SEQ, D_MODEL), f32),
        'c': nrm(ks[1], (BATCH, D_MODEL), f32),
        'ctx': nrm(ks[2], (BATCH, CTX_LEN, D_MODEL), f32),
        'c_ctx': nrm(ks[3], (D_MODEL,), f32),
        'w_ada': nrm(ks[4], (DEPTH, D_MODEL, N_MOD * D_MODEL), f32) * (0.5 * D_MODEL ** -0.5),
        'b_ada': 0.02 * nrm(ks[5], (DEPTH, N_MOD * D_MODEL), f32),
        'norm1': 1.0 + 0.05 * nrm(ks[6], (DEPTH, D_MODEL), f32),
        'norm2': 1.0 + 0.05 * nrm(ks[7], (DEPTH, D_MODEL), f32),
        'w_in': nrm(ks[10], (DEPTH, D_MODEL, IN_COLS), f32) * D_MODEL ** -0.5,
        'mlstm_gate_b': jnp.concatenate([ib, fb], axis=2),
        'mlstm_norm': 1.0 + 0.05 * nrm(ks[11], (DEPTH, M_V), f32),
        'q_norm': 1.0 + 0.05 * nrm(ks[12], (DEPTH, A_HD), f32),
        'k_norm': 1.0 + 0.05 * nrm(ks[13], (DEPTH, A_HD), f32),
        'conv_w': nrm(ks[14], (DEPTH, CONV_K, C_WIDTH), f32) * CONV_K ** -0.5,
        'w_proj_m': nrm(ks[15], (DEPTH, M_V, D_MODEL), f32) * M_V ** -0.5,
        'w_proj_a': nrm(ks[16], (DEPTH, A_Q, D_MODEL), f32) * A_Q ** -0.5,
        'w_proj_c': nrm(ks[17], (DEPTH, C_WIDTH, D_MODEL), f32) * C_WIDTH ** -0.5,
        'w_out': nrm(ks[18], (DEPTH, D_MODEL, D_MODEL), f32) * D_MODEL ** -0.5,
        'w_up': nrm(ks[19], (DEPTH, D_MODEL, D_FF), f32) * D_MODEL ** -0.5,
        'w_down': nrm(ks[20], (DEPTH, D_FF, D_MODEL), f32) * D_FF ** -0.5,
        'final_norm': 1.0 + 0.05 * nrm(ks[21], (D_MODEL,), f32),
    }


def reference(x, c, ctx, c_ctx, w_ada, b_ada, norm1, norm2, w_in, mlstm_gate_b, mlstm_norm, q_norm, k_norm,
              conv_w, w_proj_m, w_proj_a, w_proj_c, w_out, w_up, w_down, final_norm):
    cos, sin = axial_rope_tables(x.shape[1])
    for layer in range(DEPTH):
        need_ctx = layer < DEPTH - 1
        mod_l = jax.nn.silu(c) @ w_ada[layer] + b_ada[layer]
        mod_c = jax.nn.silu(c_ctx) @ w_ada[layer] + b_ada[layer]
        sh1, sc1, g1, sh2, sc2, g2 = jnp.split(mod_l[:, None, :], N_MOD, axis=-1)
        csh1, csc1, cg1, csh2, csc2, cg2 = jnp.split(mod_c, N_MOD, axis=-1)

        p_l = split_columns(modulate(rms_norm(x, norm1[layer]), sh1, sc1) @ w_in[layer])
        p_c = split_columns(modulate(rms_norm(ctx, norm1[layer]), csh1, csc1) @ w_in[layer])

        mq_l, mk_l, mv_l, mg_l = mlstm_inputs(p_l, mlstm_gate_b[layer])
        mq_c, mk_c, mv_c, mg_c = mlstm_inputs(p_c, mlstm_gate_b[layer])
        hm_l, hm_c = mlstm_bidirectional(mq_l, mk_l, mv_l, mg_l, mq_c, mk_c, mv_c, mg_c)
        ym_l = mlstm_output(hm_l, p_l[3], mlstm_norm[layer])

        aq_l = apply_rope(attn_q(p_l, q_norm[layer]), cos, sin)
        ak_l, av_l = attn_kv(p_l, k_norm[layer])
        ak_l = apply_rope(ak_l, cos, sin)
        ak_c, av_c = attn_kv(p_c, k_norm[layer])
        ya_l = latent_attention(aq_l, jnp.concatenate([ak_c, ak_l], axis=1), jnp.concatenate([av_c, av_l], axis=1))

        yc_l = conv_branch(p_l, conv_w[layer])

        x = x + g1 * merge_branches(ym_l, ya_l, yc_l, p_l, w_proj_m[layer], w_proj_a[layer], w_proj_c[layer], w_out[layer])
        x = x + g2 * squared_relu_mlp(modulate(rms_norm(x, norm2[layer]), sh2, sc2), w_up[layer], w_down[layer])

        if need_ctx:
            ym_c = mlstm_output(hm_c, p_c[3], mlstm_norm[layer])
            ya_c = context_attention(attn_q(p_c, q_norm[layer]), ak_c, av_c)
            yc_c = conv_branch(p_c, conv_w[layer])
            ctx = ctx + cg1 * merge_branches(ym_c, ya_c, yc_c, p_c, w_proj_m[layer], w_proj_a[layer], w_proj_c[layer], w_out[layer])
            ctx = ctx + cg2 * squared_relu_mlp(modulate(rms_norm(ctx, norm2[layer]), csh2, csc2), w_up[layer], w_down[layer])
    return rms_norm(x, final_norm)
```

```python
import functools

import numpy as np
import jax
import jax.numpy as jnp
from jax import lax
from jax.experimental import pallas as pl
from jax.experimental.pallas import tpu as pltpu

D_MODEL = 1024
GRID_W = 64
M_HEADS = 4
M_DQK = 128
M_DV = 256
M_QK = M_HEADS * M_DQK
M_V = M_HEADS * M_DV
M_GATES = 2 * 2 * M_HEADS
A_HEADS = 8
A_KV_HEADS = 2
A_GROUP = A_HEADS // A_KV_HEADS
A_HD = 128
A_Q = A_HEADS * A_HD
A_KV = A_KV_HEADS * A_HD
ROPE_THETA = 10000.0
C_WIDTH = D_MODEL
D_FF = 4 * D_MODEL
N_MOD = 6
EPS = 1e-6
COL_SIZES = (M_QK, M_QK, M_V, M_V, M_GATES, A_Q, A_KV, A_KV, C_WIDTH, C_WIDTH, C_WIDTH, D_MODEL, D_MODEL, D_MODEL)
COL_OFF = tuple(int(v) for v in np.concatenate([[0], np.cumsum(COL_SIZES)]))

LANES = 128
BF16_SUBLANES = 16
TOKEN_TILE = 256
M_CHUNK = 256
ATTN_KV_TILE = 768
VMEM_LIMIT = 56 * 1024 * 1024

F32 = jnp.float32
BF16 = jnp.bfloat16
NT_DIMS = (((1,), (1,)), ((), ()))


def _params(sem):
    return pltpu.CompilerParams(dimension_semantics=sem, vmem_limit_bytes=VMEM_LIMIT)


def _const_spec(shape):
    nd = len(shape)
    return pl.BlockSpec(shape, lambda *_: (0,) * nd)


def _norm_mod(x, g, shift, scale):
    ms = jnp.mean(x * x, axis=-1, keepdims=True)
    y = x * lax.rsqrt(ms + EPS) * g
    return y * (1.0 + scale) + shift


def _sigmoid(x):
    return 1.0 / (1.0 + jnp.exp(-x))


def _log_sigmoid(x):
    return jnp.minimum(x, 0.0) - jnp.log(1.0 + jnp.exp(-jnp.abs(x)))


def _mod_kernel(c_ref, w_ref, b_ref, o_ref):
    c = c_ref[...]
    s = c * _sigmoid(c)
    o_ref[...] = jnp.dot(s, w_ref[...], precision=lax.Precision.HIGHEST,
                         preferred_element_type=F32) + b_ref[...]


def _modulation(c_rows, w_ada, b_ada):
    depth = w_ada.shape[0]
    tn = 1536
    return pl.pallas_call(
        _mod_kernel,
        grid=(depth, N_MOD * D_MODEL // tn),
        in_specs=[
            pl.BlockSpec((8, D_MODEL), lambda l, j: (0, 0)),
            pl.BlockSpec((None, D_MODEL, tn), lambda l, j: (l, 0, j)),
            pl.BlockSpec((None, 1, tn), lambda l, j: (l, 0, j)),
        ],
        out_specs=pl.BlockSpec((None, 8, tn), lambda l, j: (l, 0, j)),
        out_shape=jax.ShapeDtypeStruct((depth, 8, N_MOD * D_MODEL), F32),
        compiler_params=_params(("arbitrary", "arbitrary")),
        name="modulation",
    )(c_rows, w_ada, b_ada.reshape(depth, 1, N_MOD * D_MODEL))


def _inproj_mlstm_kernel(x_ref, mod_ref, g_ref, w_ref, wkt_ref, wg_ref, wgt_ref, gb_ref, gbt_ref,
                         q_ref, k_ref, kt_ref, v_ref, o_ref, gate_ref, gatet_ref):
    h = _norm_mod(x_ref[...], g_ref[...], mod_ref[0:1, :], mod_ref[1:2, :]).astype(BF16)
    q = jnp.dot(h, w_ref[:, 0:M_QK], preferred_element_type=F32)
    q_ref[...] = (q * (M_DQK ** -0.5)).astype(BF16)
    k_ref[...] = jnp.dot(h, w_ref[:, M_QK:2 * M_QK], preferred_element_type=F32).astype(BF16)
    kt_ref[...] = lax.dot_general(wkt_ref[...], h, NT_DIMS, preferred_element_type=F32).astype(BF16)
    v_ref[...] = jnp.dot(h, w_ref[:, 2 * M_QK:2 * M_QK + M_V], preferred_element_type=F32).astype(BF16)
    o_ref[...] = jnp.dot(h, w_ref[:, 2 * M_QK + M_V:], preferred_element_type=F32).astype(BF16)
    ga = jnp.dot(h, wg_ref[...], preferred_element_type=F32) + gb_ref[...]
    lane = lax.broadcasted_iota(jnp.int32, ga.shape, 1) % LANES
    ga = jnp.where(lane >= M_HEADS, _log_sigmoid(ga), ga)
    gate_ref[0] = ga[:, :LANES]
    gate_ref[1] = ga[:, LANES:]
    gt = lax.dot_general(wgt_ref[...], h, NT_DIMS, preferred_element_type=F32) + gbt_ref[...]
    row = lax.broadcasted_iota(jnp.int32, gt.shape, 0) % (2 * M_HEADS)
    gt = jnp.where(row >= M_HEADS, _log_sigmoid(gt), gt)
    gatet_ref[0] = gt[:2 * M_HEADS]
    gatet_ref[1] = gt[2 * M_HEADS:]


def _inproj_mlstm(xc, modsel, norm_g, w, wkt, wg, wgt, gb, gbt, ntile):
    t = xc.shape[0]
    tm = TOKEN_TILE
    ncol = 2 * M_QK + 2 * M_V
    tok = lambda n: pl.BlockSpec((tm, n), lambda i: (i, 0))
    return pl.pallas_call(
        _inproj_mlstm_kernel,
        grid=(t // tm,),
        in_specs=[
            tok(D_MODEL),
            pl.BlockSpec((None, None, N_MOD, D_MODEL), lambda i: (i // ntile, jnp.minimum(i % ntile, 1), 0, 0)),
            _const_spec((1, D_MODEL)),
            _const_spec((D_MODEL, ncol)),
            _const_spec((M_QK, D_MODEL)),
            _const_spec((D_MODEL, 2 * LANES)),
            _const_spec((M_GATES, D_MODEL)),
            _const_spec((1, 2 * LANES)),
            _const_spec((M_GATES, 1)),
        ],
        out_specs=[
            tok(M_QK), tok(M_QK),
            pl.BlockSpec((M_QK, tm), lambda i: (0, i)),
            tok(M_V), tok(M_V),
            pl.BlockSpec((2, tm, LANES), lambda i: (0, i, 0)),
            pl.BlockSpec((2, 2 * M_HEADS, tm), lambda i: (0, 0, i)),
        ],
        out_shape=[
            jax.ShapeDtypeStruct((t, M_QK), BF16), jax.ShapeDtypeStruct((t, M_QK), BF16),
            jax.ShapeDtypeStruct((M_QK, t), BF16),
            jax.ShapeDtypeStruct((t, M_V), BF16), jax.ShapeDtypeStruct((t, M_V), BF16),
            jax.ShapeDtypeStruct((2, t, LANES), F32),
            jax.ShapeDtypeStruct((2, 2 * M_HEADS, t), F32),
        ],
        compiler_params=_params(("parallel",)),
        name="inproj_mlstm",
    )(xc, modsel, norm_g, w, wkt, wg, wgt, gb, gbt)


def _head_norm_rope(xh, gain, cos, sin):
    ms = jnp.mean(xh * xh, axis=-1, keepdims=True)
    y = xh * lax.rsqrt(ms + EPS) * gain
    return y * cos + pltpu.roll(y, A_HD // 2, 1) * sin


def _inproj_attn_kernel(x_ref, mod_ref, g_ref, w_ref, qg_ref, kg_ref, cos_ref, sin_ref,
                        q_ref, k_ref, v_ref):
    h = _norm_mod(x_ref[...], g_ref[...], mod_ref[0:1, :], mod_ref[1:2, :]).astype(BF16)
    cos = cos_ref[...]
    sin = sin_ref[...]
    for hd in range(A_HEADS):
        qh = jnp.dot(h, w_ref[:, hd * A_HD:(hd + 1) * A_HD], preferred_element_type=F32)
        qh = _head_norm_rope(qh, qg_ref[...], cos, sin) * (A_HD ** -0.5)
        q_ref[:, hd * A_HD:(hd + 1) * A_HD] = qh.astype(BF16)
    for hd in range(A_KV_HEADS):
        kh = jnp.dot(h, w_ref[:, A_Q + hd * A_HD:A_Q + (hd + 1) * A_HD], preferred_element_type=F32)
        k_ref[:, hd * A_HD:(hd + 1) * A_HD] = _head_norm_rope(kh, kg_ref[...], cos, sin).astype(BF16)
    v_ref[...] = jnp.dot(h, w_ref[:, A_Q + A_KV:], preferred_element_type=F32).astype(BF16)


def _inproj_attn(xc, modsel, norm_g, w, qg, kg, cos, sin, ntile):
    t = xc.shape[0]
    tm = TOKEN_TILE
    tok = lambda n: pl.BlockSpec((tm, n), lambda i: (i, 0))
    return pl.pallas_call(
        _inproj_attn_kernel,
        grid=(t // tm,),
        in_specs=[
            tok(D_MODEL),
            pl.BlockSpec((None, None, N_MOD, D_MODEL), lambda i: (i // ntile, jnp.minimum(i % ntile, 1), 0, 0)),
            _const_spec((1, D_MODEL)),
            _const_spec((D_MODEL, A_Q + 2 * A_KV)),
            _const_spec((1, A_HD)), _const_spec((1, A_HD)),
            pl.BlockSpec((tm, A_HD), lambda i: (i % ntile, 0)),
            pl.BlockSpec((tm, A_HD), lambda i: (i % ntile, 0)),
        ],
        out_specs=[tok(A_Q), tok(A_KV), tok(A_KV)],
        out_shape=[jax.ShapeDtypeStruct((t, A_Q), BF16), jax.ShapeDtypeStruct((t, A_KV), BF16),
                   jax.ShapeDtypeStruct((t, A_KV), BF16)],
        compiler_params=_params(("parallel",)),
        name="inproj_attn",
    )(xc, modsel, norm_g, w, qg, kg, cos, sin)


def _inproj_conv_kernel(x_ref, mod_ref, g_ref, w_ref, cu_ref, b_ref, sg_ref):
    h = _norm_mod(x_ref[...], g_ref[...], mod_ref[0:1, :], mod_ref[1:2, :]).astype(BF16)
    w = C_WIDTH
    u = jnp.dot(h, w_ref[:, 0:w], preferred_element_type=F32)
    cg = jnp.dot(h, w_ref[:, 2 * w:3 * w], preferred_element_type=F32)
    cu_ref[...] = (cg * u).astype(BF16)
    b_ref[...] = jnp.dot(h, w_ref[:, w:2 * w], preferred_element_type=F32).astype(BF16)
    for j in range(3):
        gj = jnp.dot(h, w_ref[:, (3 + j) * w:(4 + j) * w], preferred_element_type=F32)
        sg_ref[:, j * w:(j + 1) * w] = _sigmoid(gj).astype(BF16)


def _inproj_conv(xc, modsel, norm_g, w, ntile):
    t = xc.shape[0]
    tm = TOKEN_TILE
    tok = lambda n: pl.BlockSpec((tm, n), lambda i: (i, 0))
    return pl.pallas_call(
        _inproj_conv_kernel,
        grid=(t // tm,),
        in_specs=[
            tok(D_MODEL),
            pl.BlockSpec((None, None, N_MOD, D_MODEL), lambda i: (i // ntile, jnp.minimum(i % ntile, 1), 0, 0)),
            _const_spec((1, D_MODEL)),
            _const_spec((D_MODEL, 6 * C_WIDTH)),
        ],
        out_specs=[tok(C_WIDTH), tok(C_WIDTH), tok(3 * D_MODEL)],
        out_shape=[jax.ShapeDtypeStruct((t, C_WIDTH), BF16), jax.ShapeDtypeStruct((t, C_WIDTH), BF16),
                   jax.ShapeDtypeStruct((t, 3 * D_MODEL), BF16)],
        compiler_params=_params(("parallel",)),
        name="inproj_conv",
    )(xc, modsel, norm_g, w)


def _mlstm_kernel(q_ref, k_ref, kt_ref, v_ref, g_ref, gt_ref, h_ref, c_ref, n_ref, m_ref):
    d = pl.program_id(0)
    j = pl.program_id(2)
    L = M_CHUNK

    @pl.when(j == 0)
    def _():
        c_ref[...] = jnp.zeros_like(c_ref)
        n_ref[...] = jnp.zeros_like(n_ref)
        m_ref[...] = jnp.zeros_like(m_ref)

    sign = 1 - 2 * d
    row = lax.broadcasted_iota(jnp.int32, (L, L), 0)
    col = lax.broadcasted_iota(jnp.int32, (L, L), 1)
    mask = (row - col) * sign >= 0
    mask_t = (col - row) * sign >= 0
    g = g_ref[...]
    gt = gt_ref[...]
    hi = lax.Precision.HIGHEST
    cum_col = jnp.dot(mask.astype(F32), g, precision=hi, preferred_element_type=F32)
    cum_row = jnp.dot(gt, mask_t.astype(F32), precision=hi, preferred_element_type=F32)
    tot = jnp.sum(gt, axis=-1, keepdims=True)

    for hd in range(M_HEADS):
        fi = M_HEADS + hd
        bc = cum_col[:, fi:fi + 1]
        br = cum_row[fi:fi + 1, :]
        ic = g[:, hd:hd + 1]
        ir = gt[hd:hd + 1, :]
        total = tot[fi:fi + 1, :]
        m_old = m_ref[hd][:, 0:1]
        dmat = jnp.where(mask, bc - br + ir, -jnp.inf)
        inter = bc + m_old
        m_t = jnp.maximum(inter, jnp.max(dmat, axis=-1, keepdims=True))
        w_intra = jnp.exp(dmat - m_t)
        w_inter = jnp.exp(inter - m_t)
        qh = q_ref[:, hd * M_DQK:(hd + 1) * M_DQK]
        kh = k_ref[:, hd * M_DQK:(hd + 1) * M_DQK]
        kth = kt_ref[hd * M_DQK:(hd + 1) * M_DQK, :]
        vh = v_ref[:, hd * M_DV:(hd + 1) * M_DV]
        s = lax.dot_general(qh, kh, NT_DIMS, preferred_element_type=F32) * w_intra
        c_old = c_ref[hd]
        num = (jnp.dot(s.astype(BF16), vh, preferred_element_type=F32)
               + w_inter * jnp.dot(qh, c_old.astype(BF16), preferred_element_type=F32))
        qn = jnp.sum(qh.astype(F32) * n_ref[hd], axis=-1, keepdims=True)
        den = jnp.sum(s, axis=-1, keepdims=True) + w_inter * qn
        h_ref[:, hd * M_DV:(hd + 1) * M_DV] = num / jnp.maximum(jnp.abs(den), jnp.exp(-m_t))

        gcol = total - bc + ic
        m_new = jnp.maximum(total + m_old, jnp.max(gcol, axis=0, keepdims=True))
        w_g = jnp.exp(gcol - m_new)
        w_c = jnp.exp(total + m_old - m_new)
        wv = (w_g * vh.astype(F32)).astype(BF16)
        c_ref[hd] = w_c * c_old + jnp.dot(kth, wv, preferred_element_type=F32)
        n_ref[hd] = w_c * n_ref[hd] + jnp.sum(w_g * kh.astype(F32), axis=0, keepdims=True)
        m_ref[hd] = jnp.broadcast_to(m_new, (1, LANES))


def _mlstm(q, k, kt, v, gates, gates_t, bsz, nch):
    t = q.shape[0]
    L = M_CHUNK

    def chunk(d, b, j):
        return b * nch + jnp.where(d == 0, j, jnp.where(j == 0, 0, nch - j))

    return pl.pallas_call(
        _mlstm_kernel,
        grid=(2, bsz, nch),
        in_specs=[
            pl.BlockSpec((L, M_QK), lambda d, b, j: (chunk(d, b, j), 0)),
            pl.BlockSpec((L, M_QK), lambda d, b, j: (chunk(d, b, j), 0)),
            pl.BlockSpec((M_QK, L), lambda d, b, j: (0, chunk(d, b, j))),
            pl.BlockSpec((L, M_V), lambda d, b, j: (chunk(d, b, j), 0)),
            pl.BlockSpec((None, L, LANES), lambda d, b, j: (d, chunk(d, b, j), 0)),
            pl.BlockSpec((None, 2 * M_HEADS, L), lambda d, b, j: (d, 0, chunk(d, b, j))),
        ],
        out_specs=pl.BlockSpec((None, L, M_V), lambda d, b, j: (d, chunk(d, b, j), 0)),
        out_shape=jax.ShapeDtypeStruct((2, t, M_V), F32),
        scratch_shapes=[pltpu.VMEM((M_HEADS, M_DQK, M_DV), F32), pltpu.VMEM((M_HEADS, 1, M_DQK), F32),
                        pltpu.VMEM((M_HEADS, 1, LANES), F32)],
        compiler_params=_params(("arbitrary", "arbitrary", "arbitrary")),
        name="mlstm_scan",
    )(q, k, kt, v, gates, gates_t)


def _attn_kernel(q_ref, k_ref, v_ref, o_ref, *, nkt, tk):
    tq = q_ref.shape[0]
    for g in range(A_GROUP):
        qg = q_ref[:, g * A_HD:(g + 1) * A_HD]

        def body(c, carry):
            m, l, acc = carry
            start = pl.multiple_of(c * tk, tk)
            kc = k_ref[pl.ds(start, tk), :]
            vc = v_ref[pl.ds(start, tk), :]
            s = lax.dot_general(qg, kc, NT_DIMS, preferred_element_type=F32)
            m_new = jnp.maximum(m, jnp.max(s, axis=-1, keepdims=True))
            alpha = jnp.exp(m - m_new)
            p = jnp.exp(s - m_new)
            l = alpha * l + jnp.sum(p, axis=-1, keepdims=True)
            acc = alpha * acc + jnp.dot(p.astype(BF16), vc, preferred_element_type=F32)
            return m_new, l, acc

        init = (jnp.full((tq, 1), -jnp.inf, F32), jnp.zeros((tq, 1), F32), jnp.zeros((tq, A_HD), F32))
        _, l, acc = lax.fori_loop(0, nkt, body, init)
        o_ref[:, g * A_HD:(g + 1) * A_HD] = (acc / l).astype(BF16)


def _attention(aq, ak, av, ya_in, bsz, ntile, q_off, nq, nk):
    t = aq.shape[0]
    tq = TOKEN_TILE
    tk = min(ATTN_KV_TILE, nk)
    nt = ntile * tq
    kv_rows_per_block = nk
    assert nt % kv_rows_per_block == 0
    kvb = nt // kv_rows_per_block
    qmap = lambda b, h, i: (b * ntile + q_off + i, h)
    kmap = lambda b, h, i: (b * kvb, h)
    args = [aq, ak, av]
    in_specs = [
        pl.BlockSpec((tq, A_GROUP * A_HD), qmap),
        pl.BlockSpec((nk, A_HD), kmap),
        pl.BlockSpec((nk, A_HD), kmap),
    ]
    aliases = {}
    kern = functools.partial(_attn_kernel, nkt=nk // tk, tk=tk)
    if ya_in is not None:
        args.append(ya_in)
        in_specs.append(pl.BlockSpec(memory_space=pl.ANY))
        aliases = {3: 0}
        kern = lambda q, k, v, _y, o: _attn_kernel(q, k, v, o, nkt=nk // tk, tk=tk)
    return pl.pallas_call(
        kern,
        grid=(bsz, A_KV_HEADS, nq),
        in_specs=in_specs,
        out_specs=pl.BlockSpec((tq, A_GROUP * A_HD), qmap),
        out_shape=jax.ShapeDtypeStruct((t, A_Q), BF16),
        input_output_aliases=aliases,
        compiler_params=_params(("parallel", "parallel", "arbitrary")),
        name="gqa_attention",
    )(*args)


def _merge_kernel(x_ref, mod_ref, hf_ref, hr_ref, o_ref, mg_ref, ya_ref, cu_ref, cup_ref, cun_ref, b_ref,
                  cw_ref, sg_ref, wpm_ref, wpa_ref, wpc_ref, wo_ref, out_ref, *, ntile, tile_off):
    pos = pl.program_id(1) + tile_off
    tm = x_ref.shape[0]
    hsum = hf_ref[...] + hr_ref[...]
    parts = []
    for hd in range(M_HEADS):
        hh = hsum[:, hd * M_DV:(hd + 1) * M_DV]
        parts.append(hh * lax.rsqrt(jnp.mean(hh * hh, axis=-1, keepdims=True) + EPS))
    hn = jnp.concatenate(parts, axis=-1) * mg_ref[...]
    ym = (hn * _sigmoid(o_ref[...].astype(F32))).astype(BF16)
    has_prev = pos >= 2
    has_next = jnp.logical_and(pos >= 1, pos <= ntile - 2)
    cu = cu_ref[...].astype(F32)
    prev_row = jnp.where(has_prev, cup_ref[BF16_SUBLANES - 1:BF16_SUBLANES, :].astype(F32), 0.0)
    next_row = jnp.where(has_next, cun_ref[0:1, :].astype(F32), 0.0)
    ridx = lax.broadcasted_iota(jnp.int32, cu.shape, 0)
    up = jnp.where(ridx == 0, prev_row, pltpu.roll(cu, 1, 0))
    dn = jnp.where(ridx == tm - 1, next_row, pltpu.roll(cu, tm - 1, 0))
    conv = cw_ref[0:1, :] * up + cw_ref[1:2, :] * cu + cw_ref[2:3, :] * dn
    yc = (b_ref[...].astype(F32) * conv).astype(BF16)
    w = D_MODEL
    merged = (sg_ref[:, 0:w].astype(F32) * jnp.dot(ym, wpm_ref[...], preferred_element_type=F32)
              + sg_ref[:, w:2 * w].astype(F32) * jnp.dot(ya_ref[...], wpa_ref[...], preferred_element_type=F32)
              + sg_ref[:, 2 * w:3 * w].astype(F32) * jnp.dot(yc, wpc_ref[...], preferred_element_type=F32))
    out = jnp.dot(merged.astype(BF16), wo_ref[...], preferred_element_type=F32)
    out_ref[...] = x_ref[...] + mod_ref[2:3, :] * out


def _merge(xc, modsel, h2, o, mgain, ya, cu, bg, conv_w, sg, wpm, wpa, wpc, wo, bsz, ntile, tile_off):
    t = xc.shape[0]
    tm = TOKEN_TILE
    hb = tm // BF16_SUBLANES
    nhb = t // BF16_SUBLANES
    tile = lambda b, i: b * ntile + tile_off + i
    tok = lambda n: pl.BlockSpec((tm, n), lambda b, i: (tile(b, i), 0))
    return pl.pallas_call(
        functools.partial(_merge_kernel, ntile=ntile, tile_off=tile_off),
        grid=(bsz, ntile - tile_off),
        in_specs=[
            tok(D_MODEL),
            pl.BlockSpec((None, None, N_MOD, D_MODEL), lambda b, i: (b, jnp.minimum(i + tile_off, 1), 0, 0)),
            pl.BlockSpec((None, tm, M_V), lambda b, i: (0, tile(b, i), 0)),
            pl.BlockSpec((None, tm, M_V), lambda b, i: (1, tile(b, i), 0)),
            tok(M_V),
            _const_spec((1, M_V)),
            tok(A_Q),
            tok(C_WIDTH),
            pl.BlockSpec((BF16_SUBLANES, C_WIDTH), lambda b, i: (jnp.maximum(tile(b, i) * hb - 1, 0), 0)),
            pl.BlockSpec((BF16_SUBLANES, C_WIDTH), lambda b, i: (jnp.minimum((tile(b, i) + 1) * hb, nhb - 1), 0)),
            tok(C_WIDTH),
            _const_spec((3, C_WIDTH)),
            tok(3 * D_MODEL),
            _const_spec((M_V, D_MODEL)), _const_spec((A_Q, D_MODEL)), _const_spec((C_WIDTH, D_MODEL)),
            _const_spec((D_MODEL, D_MODEL)),
        ],
        out_specs=tok(D_MODEL),
        out_shape=jax.ShapeDtypeStruct((t, D_MODEL), F32),
        compiler_params=_params(("parallel", "parallel")),
        name="merge_branches",
    )(xc, modsel, h2, h2, o, mgain, ya, cu, cu, cu, bg, conv_w, sg, wpm, wpa, wpc, wo)


def _mlp_kernel(x_ref, mod_ref, g_ref, wup_ref, wdn_ref, out_ref):
    x = x_ref[...]
    h = _norm_mod(x, g_ref[...], mod_ref[3:4, :], mod_ref[4:5, :]).astype(BF16)
    acc = jnp.zeros(x.shape, F32)
    fc = D_FF // 4
    for c in range(4):
        a = jnp.dot(h, wup_ref[:, c * fc:(c + 1) * fc], preferred_element_type=F32)
        a = jnp.square(jnp.maximum(a, 0.0)).astype(BF16)
        acc = acc + jnp.dot(a, wdn_ref[c * fc:(c + 1) * fc, :], preferred_element_type=F32)
    out_ref[...] = x + mod_ref[5:6, :] * acc


def _mlp(xc, modsel, norm_g, wup, wdn, bsz, ntile, tile_off):
    t = xc.shape[0]
    tm = TOKEN_TILE
    tok = lambda n: pl.BlockSpec((tm, n), lambda b, i: (b * ntile + tile_off + i, 0))
    return pl.pallas_call(
        _mlp_kernel,
        grid=(bsz, ntile - tile_off),
        in_specs=[
            tok(D_MODEL),
            pl.BlockSpec((None, None, N_MOD, D_MODEL), lambda b, i: (b, jnp.minimum(i + tile_off, 1), 0, 0)),
            _const_spec((1, D_MODEL)),
            _const_spec((D_MODEL, D_FF)),
            _const_spec((D_FF, D_MODEL)),
        ],
        out_specs=tok(D_MODEL),
        out_shape=jax.ShapeDtypeStruct((t, D_MODEL), F32),
        compiler_params=_params(("parallel", "parallel")),
        name="relu2_mlp",
    )(xc, modsel, norm_g, wup, wdn)


def _final_kernel(x_ref, g_ref, o_ref):
    x = x_ref[...]
    ms = jnp.mean(x * x, axis=-1, keepdims=True)
    o_ref[...] = x * lax.rsqrt(ms + EPS) * g_ref[...]


def _final_norm(xc, g, bsz, ntile):
    tm = TOKEN_TILE
    nlat = ntile - 1
    return pl.pallas_call(
        _final_kernel,
        grid=(bsz, nlat),
        in_specs=[pl.BlockSpec((tm, D_MODEL), lambda b, i: (b * ntile + 1 + i, 0)), _const_spec((1, D_MODEL))],
        out_specs=pl.BlockSpec((None, tm, D_MODEL), lambda b, i: (b, i, 0)),
        out_shape=jax.ShapeDtypeStruct((bsz, nlat * tm, D_MODEL), F32),
        compiler_params=_params(("parallel", "parallel")),
        name="final_norm",
    )(xc, g)


def _rope_perm():
    p = np.arange(A_HD)
    part, axis, f = p // 64, (p // 32) % 2, p % 32
    return axis * 64 + part * 32 + f


def _rope_tables(n_ctx, n_lat):
    rows = n_lat // GRID_W
    row_idx = jnp.repeat(jnp.arange(rows), GRID_W)
    col_idx = jnp.tile(jnp.arange(GRID_W), rows)
    n_freq = A_HD // 4
    inv_freq = ROPE_THETA ** (-jnp.arange(n_freq, dtype=F32) / n_freq)
    pos = jnp.stack([row_idx, col_idx], axis=-1).astype(F32)
    ang = pos[:, :, None] * inv_freq
    cos = jnp.cos(ang).reshape(n_lat, 2 * n_freq)
    sin = jnp.sin(ang).reshape(n_lat, 2 * n_freq)
    cos_t = jnp.concatenate([cos, cos], axis=-1)
    sin_t = jnp.concatenate([-sin, sin], axis=-1)
    cos_t = jnp.concatenate([jnp.ones((n_ctx, A_HD), F32), cos_t], axis=0)
    sin_t = jnp.concatenate([jnp.zeros((n_ctx, A_HD), F32), sin_t], axis=0)
    return cos_t, sin_t


def kernel(x, c, ctx, c_ctx, w_ada, b_ada, norm1, norm2, w_in, mlstm_gate_b, mlstm_norm, q_norm, k_norm,
           conv_w, w_proj_m, w_proj_a, w_proj_c, w_out, w_up, w_down, final_norm):
    bsz, n_lat, _ = x.shape
    n_ctx = ctx.shape[1]
    depth = w_in.shape[0]
    assert n_ctx == TOKEN_TILE and n_ctx == M_CHUNK and n_lat % TOKEN_TILE == 0 and bsz + 1 <= 8
    nt = n_ctx + n_lat
    ntile = nt // TOKEN_TILE
    nch = nt // M_CHUNK

    xc = jnp.concatenate([ctx, x], axis=1).reshape(bsz * nt, D_MODEL)
    c_rows = jnp.zeros((8, D_MODEL), F32).at[:bsz].set(c).at[bsz].set(c_ctx)
    mods = _modulation(c_rows, w_ada, b_ada).reshape(depth, 8, N_MOD, D_MODEL)
    cos_t, sin_t = _rope_tables(n_ctx, n_lat)
    perm = _rope_perm()
    head_perm = lambda n: np.concatenate([h * A_HD + perm for h in range(n)])

    ya = None
    for layer in range(depth):
        need_ctx = layer < depth - 1
        tile_off = 0 if need_ctx else 1
        ml = mods[layer]
        modsel = jnp.stack([jnp.broadcast_to(ml[bsz], (bsz, N_MOD, D_MODEL)), ml[:bsz]], axis=1)
        wl = w_in[layer]
        o = COL_OFF
        w_m = jnp.concatenate([wl[:, o[0]:o[4]]], axis=1).astype(BF16)
        w_kt = wl[:, o[1]:o[2]].T.astype(BF16)
        w_g = wl[:, o[4]:o[5]]
        w_gp = jnp.zeros((D_MODEL, 2 * LANES), F32)
        w_gp = w_gp.at[:, 0:2 * M_HEADS].set(w_g[:, 0:2 * M_HEADS])
        w_gp = w_gp.at[:, LANES:LANES + 2 * M_HEADS].set(w_g[:, 2 * M_HEADS:]).astype(BF16)
        w_gt = w_g.T.astype(BF16)
        gb = mlstm_gate_b[layer].reshape(M_GATES)
        gbp = jnp.zeros((1, 2 * LANES), F32)
        gbp = gbp.at[0, 0:2 * M_HEADS].set(gb[0:2 * M_HEADS]).at[0, LANES:LANES + 2 * M_HEADS].set(gb[2 * M_HEADS:])
        gbt = gb.reshape(M_GATES, 1)
        w_a = jnp.concatenate([wl[:, o[5]:o[6]][:, head_perm(A_HEADS)], wl[:, o[6]:o[7]][:, head_perm(A_KV_HEADS)],
                               wl[:, o[7]:o[8]]], axis=1).astype(BF16)
        w_c = wl[:, o[8]:o[14]].astype(BF16)
        n1 = norm1[layer].reshape(1, D_MODEL)
        n2 = norm2[layer].reshape(1, D_MODEL)

        mq, mk, mkt, mv, mo, gates, gates_t = _inproj_mlstm(xc, modsel, n1, w_m, w_kt, w_gp, w_gt, gbp, gbt, ntile)
        aq, ak, av = _inproj_attn(xc, modsel, n1, w_a, q_norm[layer][perm].reshape(1, A_HD),
                                  k_norm[layer][perm].reshape(1, A_HD), cos_t, sin_t, ntile)
        cu, bg, sg = _inproj_conv(xc, modsel, n1, w_c, ntile)

        h2 = _mlstm(mq, mk, mkt, mv, gates, gates_t, bsz, nch)
        ya = _attention(aq, ak, av, None, bsz, ntile, 1, ntile - 1, nt)
        if need_ctx:
            ya = _attention(aq, ak, av, ya, bsz, ntile, 0, 1, n_ctx)

        xc = _merge(xc, modsel, h2, mo, mlstm_norm[layer].reshape(1, M_V), ya, cu, bg, conv_w[layer], sg,
                    w_proj_m[layer].astype(BF16), w_proj_a[layer].astype(BF16), w_proj_c[layer].astype(BF16),
                    w_out[layer].astype(BF16), bsz, ntile, tile_off)
        xc = _mlp(xc, modsel, n2, w_up[layer].astype(BF16), w_down[layer].astype(BF16), bsz, ntile, tile_off)

    return _final_norm(xc, final_norm.reshape(1, D_MODEL), bsz, ntile)
```

```python
import functools

import numpy as np
import jax
import jax.numpy as jnp
from jax import lax
from jax.experimental import pallas as pl
from jax.experimental.pallas import tpu as pltpu

D_MODEL = 1024
GRID_W = 64
M_HEADS = 4
M_DQK = 128
M_DV = 256
M_QK = M_HEADS * M_DQK
M_V = M_HEADS * M_DV
M_GATES = 2 * 2 * M_HEADS
A_HEADS = 8
A_KV_HEADS = 2
A_GROUP = A_HEADS // A_KV_HEADS
A_HD = 128
A_Q = A_HEADS * A_HD
A_KV = A_KV_HEADS * A_HD
ROPE_THETA = 10000.0
C_WIDTH = D_MODEL
D_FF = 4 * D_MODEL
N_MOD = 6
EPS = 1e-6
COL_SIZES = (M_QK, M_QK, M_V, M_V, M_GATES, A_Q, A_KV, A_KV, C_WIDTH, C_WIDTH, C_WIDTH, D_MODEL, D_MODEL, D_MODEL)
COL_OFF = tuple(int(v) for v in np.concatenate([[0], np.cumsum(COL_SIZES)]))

LANES = 128
BF16_SUBLANES = 16
TOKEN_TILE = 256
M_CHUNK = 256
ATTN_KV_TILE = 768
VMEM_LIMIT = 56 * 1024 * 1024

F32 = jnp.float32
LOG2_E = 1.4426950408889634
BF16 = jnp.bfloat16
NT_DIMS = (((1,), (1,)), ((), ()))


def _params(sem):
    return pltpu.CompilerParams(dimension_semantics=sem, vmem_limit_bytes=VMEM_LIMIT)


def _const_spec(shape):
    nd = len(shape)
    return pl.BlockSpec(shape, lambda *_: (0,) * nd)


def _norm_mod(x, g, shift, scale):
    ms = jnp.mean(x * x, axis=-1, keepdims=True)
    y = x * lax.rsqrt(ms + EPS) * g
    return y * (1.0 + scale) + shift


def _sigmoid(x):
    return 1.0 / (1.0 + jnp.exp(-x))


def _log_sigmoid(x):
    return jnp.minimum(x, 0.0) - jnp.log(1.0 + jnp.exp(-jnp.abs(x)))


def _mod_kernel(c_ref, w_ref, b_ref, o_ref):
    c = c_ref[...]
    s = c * _sigmoid(c)
    o_ref[...] = jnp.dot(s, w_ref[...], precision=lax.Precision.HIGHEST,
                         preferred_element_type=F32) + b_ref[...]


def _modulation(c_rows, w_ada, b_ada):
    depth = w_ada.shape[0]
    tn = 1536
    return pl.pallas_call(
        _mod_kernel,
        grid=(depth, N_MOD * D_MODEL // tn),
        in_specs=[
            pl.BlockSpec((8, D_MODEL), lambda l, j: (0, 0)),
            pl.BlockSpec((None, D_MODEL, tn), lambda l, j: (l, 0, j)),
            pl.BlockSpec((None, 1, tn), lambda l, j: (l, 0, j)),
        ],
        out_specs=pl.BlockSpec((None, 8, tn), lambda l, j: (l, 0, j)),
        out_shape=jax.ShapeDtypeStruct((depth, 8, N_MOD * D_MODEL), F32),
        compiler_params=_params(("arbitrary", "arbitrary")),
        name="modulation",
    )(c_rows, w_ada, b_ada.reshape(depth, 1, N_MOD * D_MODEL))


def _inproj_mlstm_kernel(x_ref, mod_ref, g_ref, w_ref, wkt_ref, wg_ref, wgt_ref, gb_ref, gbt_ref,
                         q_ref, k_ref, kt_ref, v_ref, o_ref, gate_ref, gatet_ref):
    h = _norm_mod(x_ref[...], g_ref[...], mod_ref[0:1, :], mod_ref[1:2, :]).astype(BF16)
    q = jnp.dot(h, w_ref[:, 0:M_QK], preferred_element_type=F32)
    q_ref[...] = (q * (M_DQK ** -0.5)).astype(BF16)
    k_ref[...] = jnp.dot(h, w_ref[:, M_QK:2 * M_QK], preferred_element_type=F32).astype(BF16)
    kt_ref[...] = lax.dot_general(wkt_ref[...], h, NT_DIMS, preferred_element_type=F32).astype(BF16)
    v_ref[...] = jnp.dot(h, w_ref[:, 2 * M_QK:2 * M_QK + M_V], preferred_element_type=F32).astype(BF16)
    o_ref[...] = jnp.dot(h, w_ref[:, 2 * M_QK + M_V:], preferred_element_type=F32).astype(BF16)
    ga = jnp.dot(h, wg_ref[...], preferred_element_type=F32) + gb_ref[...]
    lane = lax.broadcasted_iota(jnp.int32, ga.shape, 1) % LANES
    ga = jnp.where(lane >= M_HEADS, _log_sigmoid(ga), ga)
    gate_ref[0] = ga[:, :LANES]
    gate_ref[1] = ga[:, LANES:]
    gt = lax.dot_general(wgt_ref[...], h, NT_DIMS, preferred_element_type=F32) + gbt_ref[...]
    row = lax.broadcasted_iota(jnp.int32, gt.shape, 0) % (2 * M_HEADS)
    gt = jnp.where(row >= M_HEADS, _log_sigmoid(gt), gt)
    gatet_ref[0] = gt[:2 * M_HEADS]
    gatet_ref[1] = gt[2 * M_HEADS:]


def _inproj_mlstm(xc, modsel, norm_g, w, wkt, wg, wgt, gb, gbt, ntile):
    t = xc.shape[0]
    tm = TOKEN_TILE
    ncol = 2 * M_QK + 2 * M_V
    tok = lambda n: pl.BlockSpec((tm, n), lambda i: (i, 0))
    return pl.pallas_call(
        _inproj_mlstm_kernel,
        grid=(t // tm,),
        in_specs=[
            tok(D_MODEL),
            pl.BlockSpec((None, None, N_MOD, D_MODEL), lambda i: (i // ntile, jnp.minimum(i % ntile, 1), 0, 0)),
            _const_spec((1, D_MODEL)),
            _const_spec((D_MODEL, ncol)),
            _const_spec((M_QK, D_MODEL)),
            _const_spec((D_MODEL, 2 * LANES)),
            _const_spec((M_GATES, D_MODEL)),
            _const_spec((1, 2 * LANES)),
            _const_spec((M_GATES, 1)),
        ],
        out_specs=[
            tok(M_QK), tok(M_QK),
            pl.BlockSpec((M_QK, tm), lambda i: (0, i)),
            tok(M_V), tok(M_V),
            pl.BlockSpec((2, tm, LANES), lambda i: (0, i, 0)),
            pl.BlockSpec((2, 2 * M_HEADS, tm), lambda i: (0, 0, i)),
        ],
        out_shape=[
            jax.ShapeDtypeStruct((t, M_QK), BF16), jax.ShapeDtypeStruct((t, M_QK), BF16),
            jax.ShapeDtypeStruct((M_QK, t), BF16),
            jax.ShapeDtypeStruct((t, M_V), BF16), jax.ShapeDtypeStruct((t, M_V), BF16),
            jax.ShapeDtypeStruct((2, t, LANES), F32),
            jax.ShapeDtypeStruct((2, 2 * M_HEADS, t), F32),
        ],
        compiler_params=_params(("parallel",)),
        name="inproj_mlstm",
    )(xc, modsel, norm_g, w, wkt, wg, wgt, gb, gbt)


def _head_norm_rope(xh, gain, cos, sin):
    ms = jnp.mean(xh * xh, axis=-1, keepdims=True)
    y = xh * lax.rsqrt(ms + EPS) * gain
    return y * cos + pltpu.roll(y, A_HD // 2, 1) * sin


def _inproj_attn_kernel(x_ref, mod_ref, g_ref, w_ref, wvt_ref, qg_ref, kg_ref, cos_ref, sin_ref,
                        q_ref, k_ref, vt_ref):
    h = _norm_mod(x_ref[...], g_ref[...], mod_ref[0:1, :], mod_ref[1:2, :]).astype(BF16)
    cos = cos_ref[...]
    sin = sin_ref[...]
    for hd in range(A_HEADS):
        qh = jnp.dot(h, w_ref[:, hd * A_HD:(hd + 1) * A_HD], preferred_element_type=F32)
        qh = _head_norm_rope(qh, qg_ref[...], cos, sin) * (A_HD ** -0.5 * LOG2_E)
        q_ref[:, hd * A_HD:(hd + 1) * A_HD] = qh.astype(BF16)
    for hd in range(A_KV_HEADS):
        kh = jnp.dot(h, w_ref[:, A_Q + hd * A_HD:A_Q + (hd + 1) * A_HD], preferred_element_type=F32)
        k_ref[:, hd * A_HD:(hd + 1) * A_HD] = _head_norm_rope(kh, kg_ref[...], cos, sin).astype(BF16)
    vt_ref[...] = lax.dot_general(wvt_ref[...], h, NT_DIMS, preferred_element_type=F32).astype(BF16)


def _inproj_attn(xc, modsel, norm_g, w, wvt, qg, kg, cos, sin, ntile):
    t = xc.shape[0]
    tm = TOKEN_TILE
    tok = lambda n: pl.BlockSpec((tm, n), lambda i: (i, 0))
    return pl.pallas_call(
        _inproj_attn_kernel,
        grid=(t // tm,),
        in_specs=[
            tok(D_MODEL),
            pl.BlockSpec((None, None, N_MOD, D_MODEL), lambda i: (i // ntile, jnp.minimum(i % ntile, 1), 0, 0)),
            _const_spec((1, D_MODEL)),
            _const_spec((D_MODEL, A_Q + A_KV)),
            _const_spec((A_KV, D_MODEL)),
            _const_spec((1, A_HD)), _const_spec((1, A_HD)),
            pl.BlockSpec((tm, A_HD), lambda i: (i % ntile, 0)),
            pl.BlockSpec((tm, A_HD), lambda i: (i % ntile, 0)),
        ],
        out_specs=[tok(A_Q), tok(A_KV), pl.BlockSpec((None, A_KV, tm), lambda i: (i, 0, 0))],
        out_shape=[jax.ShapeDtypeStruct((t, A_Q), BF16), jax.ShapeDtypeStruct((t, A_KV), BF16),
                   jax.ShapeDtypeStruct((t // tm, A_KV, tm), BF16)],
        compiler_params=_params(("parallel",)),
        name="inproj_attn",
    )(xc, modsel, norm_g, w, wvt, qg, kg, cos, sin)


def _inproj_conv_kernel(x_ref, mod_ref, g_ref, w_ref, cu_ref, b_ref, sg_ref):
    h = _norm_mod(x_ref[...], g_ref[...], mod_ref[0:1, :], mod_ref[1:2, :]).astype(BF16)
    w = C_WIDTH
    u = jnp.dot(h, w_ref[:, 0:w], preferred_element_type=F32)
    cg = jnp.dot(h, w_ref[:, 2 * w:3 * w], preferred_element_type=F32)
    cu_ref[...] = (cg * u).astype(BF16)
    b_ref[...] = jnp.dot(h, w_ref[:, w:2 * w], preferred_element_type=F32).astype(BF16)
    for j in range(3):
        gj = jnp.dot(h, w_ref[:, (3 + j) * w:(4 + j) * w], preferred_element_type=F32)
        sg_ref[:, j * w:(j + 1) * w] = _sigmoid(gj).astype(BF16)


def _inproj_conv(xc, modsel, norm_g, w, ntile):
    t = xc.shape[0]
    tm = TOKEN_TILE
    tok = lambda n: pl.BlockSpec((tm, n), lambda i: (i, 0))
    return pl.pallas_call(
        _inproj_conv_kernel,
        grid=(t // tm,),
        in_specs=[
            tok(D_MODEL),
            pl.BlockSpec((None, None, N_MOD, D_MODEL), lambda i: (i // ntile, jnp.minimum(i % ntile, 1), 0, 0)),
            _const_spec((1, D_MODEL)),
            _const_spec((D_MODEL, 6 * C_WIDTH)),
        ],
        out_specs=[tok(C_WIDTH), tok(C_WIDTH), tok(3 * D_MODEL)],
        out_shape=[jax.ShapeDtypeStruct((t, C_WIDTH), BF16), jax.ShapeDtypeStruct((t, C_WIDTH), BF16),
                   jax.ShapeDtypeStruct((t, 3 * D_MODEL), BF16)],
        compiler_params=_params(("parallel",)),
        name="inproj_conv",
    )(xc, modsel, norm_g, w)


def _mlstm_kernel(q_ref, k_ref, kt_ref, v_ref, g_ref, gt_ref, h_ref, c_ref, n_ref, m_ref):
    d = pl.program_id(0)
    j = pl.program_id(2)
    L = M_CHUNK

    @pl.when(j == 0)
    def _():
        c_ref[...] = jnp.zeros_like(c_ref)
        n_ref[...] = jnp.zeros_like(n_ref)
        m_ref[...] = jnp.zeros_like(m_ref)

    sign = 1 - 2 * d
    row = lax.broadcasted_iota(jnp.int32, (L, L), 0)
    col = lax.broadcasted_iota(jnp.int32, (L, L), 1)
    mask = (row - col) * sign >= 0
    mask_t = (col - row) * sign >= 0
    g = g_ref[...]
    gt = gt_ref[...]
    hi = lax.Precision.HIGHEST
    cum_col = jnp.dot(mask.astype(F32), g, precision=hi, preferred_element_type=F32)
    cum_row = jnp.dot(gt, mask_t.astype(F32), precision=hi, preferred_element_type=F32)
    tot = jnp.sum(gt, axis=-1, keepdims=True)

    for hd in range(M_HEADS):
        fi = M_HEADS + hd
        bc = cum_col[:, fi:fi + 1]
        br = cum_row[fi:fi + 1, :]
        ic = g[:, hd:hd + 1]
        ir = gt[hd:hd + 1, :]
        total = tot[fi:fi + 1, :]
        m_old = m_ref[hd][:, 0:1]
        dmat = jnp.where(mask, bc - br + ir, -jnp.inf)
        inter = bc + m_old
        m_t = jnp.maximum(inter, jnp.max(dmat, axis=-1, keepdims=True))
        w_intra = jnp.exp(dmat - m_t)
        w_inter = jnp.exp(inter - m_t)
        qh = q_ref[:, hd * M_DQK:(hd + 1) * M_DQK]
        kh = k_ref[:, hd * M_DQK:(hd + 1) * M_DQK]
        kth = kt_ref[hd * M_DQK:(hd + 1) * M_DQK, :]
        vh = v_ref[:, hd * M_DV:(hd + 1) * M_DV]
        s = lax.dot_general(qh, kh, NT_DIMS, preferred_element_type=F32) * w_intra
        c_old = c_ref[hd]
        num = (jnp.dot(s.astype(BF16), vh, preferred_element_type=F32)
               + w_inter * jnp.dot(qh, c_old.astype(BF16), preferred_element_type=F32))
        qn = jnp.sum(qh.astype(F32) * n_ref[hd], axis=-1, keepdims=True)
        den = jnp.sum(s, axis=-1, keepdims=True) + w_inter * qn
        h_ref[:, hd * M_DV:(hd + 1) * M_DV] = num / jnp.maximum(jnp.abs(den), jnp.exp(-m_t))

        gcol = total - bc + ic
        m_new = jnp.maximum(total + m_old, jnp.max(gcol, axis=0, keepdims=True))
        w_g = jnp.exp(gcol - m_new)
        w_c = jnp.exp(total + m_old - m_new)
        wv = (w_g * vh.astype(F32)).astype(BF16)
        c_ref[hd] = w_c * c_old + jnp.dot(kth, wv, preferred_element_type=F32)
        n_ref[hd] = w_c * n_ref[hd] + jnp.sum(w_g * kh.astype(F32), axis=0, keepdims=True)
        m_ref[hd] = jnp.broadcast_to(m_new, (1, LANES))


def _mlstm(q, k, kt, v, gates, gates_t, bsz, nch):
    t = q.shape[0]
    L = M_CHUNK

    def chunk(d, b, j):
        return b * nch + jnp.where(d == 0, j, jnp.where(j == 0, 0, nch - j))

    return pl.pallas_call(
        _mlstm_kernel,
        grid=(2, bsz, nch),
        in_specs=[
            pl.BlockSpec((L, M_QK), lambda d, b, j: (chunk(d, b, j), 0)),
            pl.BlockSpec((L, M_QK), lambda d, b, j: (chunk(d, b, j), 0)),
            pl.BlockSpec((M_QK, L), lambda d, b, j: (0, chunk(d, b, j))),
            pl.BlockSpec((L, M_V), lambda d, b, j: (chunk(d, b, j), 0)),
            pl.BlockSpec((None, L, LANES), lambda d, b, j: (d, chunk(d, b, j), 0)),
            pl.BlockSpec((None, 2 * M_HEADS, L), lambda d, b, j: (d, 0, chunk(d, b, j))),
        ],
        out_specs=pl.BlockSpec((None, L, M_V), lambda d, b, j: (d, chunk(d, b, j), 0)),
        out_shape=jax.ShapeDtypeStruct((2, t, M_V), F32),
        scratch_shapes=[pltpu.VMEM((M_HEADS, M_DQK, M_DV), F32), pltpu.VMEM((M_HEADS, 1, M_DQK), F32),
                        pltpu.VMEM((M_HEADS, 1, LANES), F32)],
        compiler_params=_params(("arbitrary", "arbitrary", "arbitrary")),
        name="mlstm_scan",
    )(q, k, kt, v, gates, gates_t)


def _attn_kernel(*refs, nkt, sub, aliased):
    if aliased:
        q_ref, k_ref, vt_ref, _, o_ref, acc_ref, s_ref = refs
    else:
        q_ref, k_ref, vt_ref, o_ref, acc_ref, s_ref = refs
    tq = q_ref.shape[0]
    tt = vt_ref.shape[2]
    tk = sub * tt
    qs = [q_ref[:, g * A_HD:(g + 1) * A_HD] for g in range(A_GROUP)]
    acc_ref[...] = jnp.zeros_like(acc_ref)

    def scores(c, g, j, slot):
        kj = k_ref[pl.ds(pl.multiple_of(c * tk + j * tt, tt), tt), :]
        s = lax.dot_general(kj, qs[g], NT_DIMS, preferred_element_type=F32)
        s_ref[slot, j * tt:(j + 1) * tt, :] = s
        return jnp.max(s.reshape(tt // 8, 8, tq), axis=0)

    def colmax(cm8):
        return jnp.max(cm8, axis=0, keepdims=True)

    cm8 = scores(0, 0, 0, 0)
    for j in range(1, sub):
        cm8 = jnp.maximum(cm8, scores(0, 0, j, 0))

    def body(c, carry):
        ms, ls, cm = list(carry[0]), list(carry[1]), carry[2]
        c_next = jnp.minimum(c + 1, nkt - 1)
        for g in range(A_GROUP):
            cn, gn = (c, g + 1) if g + 1 < A_GROUP else (c_next, 0)
            m_new = jnp.maximum(ms[g], cm)
            alpha = jnp.exp2(ms[g] - m_new)
            l8 = None
            pv = None
            cm8n = None
            for j in range(sub):
                c8 = scores(cn, gn, j, (g + 1) % 2)
                cm8n = c8 if cm8n is None else jnp.maximum(cm8n, c8)
                p = jnp.exp2(s_ref[g % 2, j * tt:(j + 1) * tt, :] - m_new)
                p8 = jnp.sum(p.reshape(tt // 8, 8, tq), axis=0)
                l8 = p8 if l8 is None else l8 + p8
                d = jnp.dot(vt_ref[c * sub + j], p.astype(BF16), preferred_element_type=F32)
                pv = d if pv is None else pv + d
            acc_ref[g] = alpha * acc_ref[g] + pv
            ls[g] = alpha * ls[g] + l8
            ms[g] = m_new
            cm = colmax(cm8n)
        return tuple(ms), tuple(ls), cm

    init = (tuple(jnp.full((1, tq), -jnp.inf, F32) for _ in range(A_GROUP)),
            tuple(jnp.zeros((8, tq), F32) for _ in range(A_GROUP)), colmax(cm8))
    _, ls, _ = lax.fori_loop(0, nkt, body, init)
    for g in range(A_GROUP):
        l = jnp.sum(ls[g], axis=0, keepdims=True)
        o_ref[:, g * A_HD:(g + 1) * A_HD] = (acc_ref[g] / l).T.astype(BF16)


def _attention(aq, ak, avt, ya_in, bsz, ntile, q_off, nq, nk):
    t = aq.shape[0]
    tq = TOKEN_TILE
    tt = avt.shape[2]
    sub = min(ATTN_KV_TILE, nk) // tt
    tk = sub * tt
    nt = ntile * tq
    assert nt % nk == 0 and nk % tk == 0
    qmap = lambda b, h, i: (b * ntile + q_off + i, h)
    args = [aq, ak, avt]
    in_specs = [
        pl.BlockSpec((tq, A_GROUP * A_HD), qmap),
        pl.BlockSpec((nk, A_HD), lambda b, h, i: (b * (nt // nk), h)),
        pl.BlockSpec((nk // tt, A_HD, tt), lambda b, h, i: (b * (nt // nk), h, 0)),
    ]
    aliases = {}
    if ya_in is not None:
        args.append(ya_in)
        in_specs.append(pl.BlockSpec(memory_space=pl.ANY))
        aliases = {3: 0}
    return pl.pallas_call(
        functools.partial(_attn_kernel, nkt=nk // tk, sub=sub, aliased=ya_in is not None),
        grid=(bsz, A_KV_HEADS, nq),
        in_specs=in_specs,
        out_specs=pl.BlockSpec((tq, A_GROUP * A_HD), qmap),
        out_shape=jax.ShapeDtypeStruct((t, A_Q), BF16),
        scratch_shapes=[pltpu.VMEM((A_GROUP, A_HD, tq), F32), pltpu.VMEM((2, tk, tq), F32)],
        input_output_aliases=aliases,
        compiler_params=_params(("parallel", "parallel", "arbitrary")),
        name="gqa_attention",
    )(*args)


def _merge_kernel(x_ref, mod_ref, hf_ref, hr_ref, o_ref, mg_ref, ya_ref, cu_ref, cup_ref, cun_ref, b_ref,
                  cw_ref, sg_ref, wpm_ref, wpa_ref, wpc_ref, wo_ref, out_ref, *, ntile, tile_off):
    pos = pl.program_id(1) + tile_off
    tm = x_ref.shape[0]
    hsum = hf_ref[...] + hr_ref[...]
    parts = []
    for hd in range(M_HEADS):
        hh = hsum[:, hd * M_DV:(hd + 1) * M_DV]
        parts.append(hh * lax.rsqrt(jnp.mean(hh * hh, axis=-1, keepdims=True) + EPS))
    hn = jnp.concatenate(parts, axis=-1) * mg_ref[...]
    ym = (hn * _sigmoid(o_ref[...].astype(F32))).astype(BF16)
    has_prev = pos >= 2
    has_next = jnp.logical_and(pos >= 1, pos <= ntile - 2)
    cu = cu_ref[...].astype(F32)
    prev_row = jnp.where(has_prev, cup_ref[BF16_SUBLANES - 1:BF16_SUBLANES, :].astype(F32), 0.0)
    next_row = jnp.where(has_next, cun_ref[0:1, :].astype(F32), 0.0)
    ridx = lax.broadcasted_iota(jnp.int32, cu.shape, 0)
    up = jnp.where(ridx == 0, prev_row, pltpu.roll(cu, 1, 0))
    dn = jnp.where(ridx == tm - 1, next_row, pltpu.roll(cu, tm - 1, 0))
    conv = cw_ref[0:1, :] * up + cw_ref[1:2, :] * cu + cw_ref[2:3, :] * dn
    yc = (b_ref[...].astype(F32) * conv).astype(BF16)
    w = D_MODEL
    merged = (sg_ref[:, 0:w].astype(F32) * jnp.dot(ym, wpm_ref[...], preferred_element_type=F32)
              + sg_ref[:, w:2 * w].astype(F32) * jnp.dot(ya_ref[...], wpa_ref[...], preferred_element_type=F32)
              + sg_ref[:, 2 * w:3 * w].astype(F32) * jnp.dot(yc, wpc_ref[...], preferred_element_type=F32))
    out = jnp.dot(merged.astype(BF16), wo_ref[...], preferred_element_type=F32)
    out_ref[...] = x_ref[...] + mod_ref[2:3, :] * out


def _merge(xc, modsel, h2, o, mgain, ya, cu, bg, conv_w, sg, wpm, wpa, wpc, wo, bsz, ntile, tile_off):
    t = xc.shape[0]
    tm = TOKEN_TILE
    hb = tm // BF16_SUBLANES
    nhb = t // BF16_SUBLANES
    tile = lambda b, i: b * ntile + tile_off + i
    tok = lambda n: pl.BlockSpec((tm, n), lambda b, i: (tile(b, i), 0))
    return pl.pallas_call(
        functools.partial(_merge_kernel, ntile=ntile, tile_off=tile_off),
        grid=(bsz, ntile - tile_off),
        in_specs=[
            tok(D_MODEL),
            pl.BlockSpec((None, None, N_MOD, D_MODEL), lambda b, i: (b, jnp.minimum(i + tile_off, 1), 0, 0)),
            pl.BlockSpec((None, tm, M_V), lambda b, i: (0, tile(b, i), 0)),
            pl.BlockSpec((None, tm, M_V), lambda b, i: (1, tile(b, i), 0)),
            tok(M_V),
            _const_spec((1, M_V)),
            tok(A_Q),
            tok(C_WIDTH),
            pl.BlockSpec((BF16_SUBLANES, C_WIDTH), lambda b, i: (jnp.maximum(tile(b, i) * hb - 1, 0), 0)),
            pl.BlockSpec((BF16_SUBLANES, C_WIDTH), lambda b, i: (jnp.minimum((tile(b, i) + 1) * hb, nhb - 1), 0)),
            tok(C_WIDTH),
            _const_spec((3, C_WIDTH)),
            tok(3 * D_MODEL),
            _const_spec((M_V, D_MODEL)), _const_spec((A_Q, D_MODEL)), _const_spec((C_WIDTH, D_MODEL)),
            _const_spec((D_MODEL, D_MODEL)),
        ],
        out_specs=tok(D_MODEL),
        out_shape=jax.ShapeDtypeStruct((t, D_MODEL), F32),
        compiler_params=_params(("parallel", "parallel")),
        name="merge_branches",
    )(xc, modsel, h2, h2, o, mgain, ya, cu, cu, cu, bg, conv_w, sg, wpm, wpa, wpc, wo)


def _mlp_kernel(x_ref, mod_ref, g_ref, wup_ref, wdn_ref, out_ref):
    x = x_ref[...]
    h = _norm_mod(x, g_ref[...], mod_ref[3:4, :], mod_ref[4:5, :]).astype(BF16)
    acc = jnp.zeros(x.shape, F32)
    fc = D_FF // 4
    for c in range(4):
        a = jnp.dot(h, wup_ref[:, c * fc:(c + 1) * fc], preferred_element_type=F32)
        a = jnp.square(jnp.maximum(a, 0.0)).astype(BF16)
        acc = acc + jnp.dot(a, wdn_ref[c * fc:(c + 1) * fc, :], preferred_element_type=F32)
    out_ref[...] = x + mod_ref[5:6, :] * acc


def _mlp(xc, modsel, norm_g, wup, wdn, bsz, ntile, tile_off):
    t = xc.shape[0]
    tm = TOKEN_TILE
    tok = lambda n: pl.BlockSpec((tm, n), lambda b, i: (b * ntile + tile_off + i, 0))
    return pl.pallas_call(
        _mlp_kernel,
        grid=(bsz, ntile - tile_off),
        in_specs=[
            tok(D_MODEL),
            pl.BlockSpec((None, None, N_MOD, D_MODEL), lambda b, i: (b, jnp.minimum(i + tile_off, 1), 0, 0)),
            _const_spec((1, D_MODEL)),
            _const_spec((D_MODEL, D_FF)),
            _const_spec((D_FF, D_MODEL)),
        ],
        out_specs=tok(D_MODEL),
        out_shape=jax.ShapeDtypeStruct((t, D_MODEL), F32),
        compiler_params=_params(("parallel", "parallel")),
        name="relu2_mlp",
    )(xc, modsel, norm_g, wup, wdn)


def _final_kernel(x_ref, g_ref, o_ref):
    x = x_ref[...]
    ms = jnp.mean(x * x, axis=-1, keepdims=True)
    o_ref[...] = x * lax.rsqrt(ms + EPS) * g_ref[...]


def _final_norm(xc, g, bsz, ntile):
    tm = TOKEN_TILE
    nlat = ntile - 1
    return pl.pallas_call(
        _final_kernel,
        grid=(bsz, nlat),
        in_specs=[pl.BlockSpec((tm, D_MODEL), lambda b, i: (b * ntile + 1 + i, 0)), _const_spec((1, D_MODEL))],
        out_specs=pl.BlockSpec((None, tm, D_MODEL), lambda b, i: (b, i, 0)),
        out_shape=jax.ShapeDtypeStruct((bsz, nlat * tm, D_MODEL), F32),
        compiler_params=_params(("parallel", "parallel")),
        name="final_norm",
    )(xc, g)


def _rope_perm():
    p = np.arange(A_HD)
    part, axis, f = p // 64, (p // 32) % 2, p % 32
    return axis * 64 + part * 32 + f


def _rope_tables(n_ctx, n_lat):
    rows = n_lat // GRID_W
    row_idx = jnp.repeat(jnp.arange(rows), GRID_W)
    col_idx = jnp.tile(jnp.arange(GRID_W), rows)
    n_freq = A_HD // 4
    inv_freq = ROPE_THETA ** (-jnp.arange(n_freq, dtype=F32) / n_freq)
    pos = jnp.stack([row_idx, col_idx], axis=-1).astype(F32)
    ang = pos[:, :, None] * inv_freq
    cos = jnp.cos(ang).reshape(n_lat, 2 * n_freq)
    sin = jnp.sin(ang).reshape(n_lat, 2 * n_freq)
    cos_t = jnp.concatenate([cos, cos], axis=-1)
    sin_t = jnp.concatenate([-sin, sin], axis=-1)
    cos_t = jnp.concatenate([jnp.ones((n_ctx, A_HD), F32), cos_t], axis=0)
    sin_t = jnp.concatenate([jnp.zeros((n_ctx, A_HD), F32), sin_t], axis=0)
    return cos_t, sin_t


def kernel(x, c, ctx, c_ctx, w_ada, b_ada, norm1, norm2, w_in, mlstm_gate_b, mlstm_norm, q_norm, k_norm,
           conv_w, w_proj_m, w_proj_a, w_proj_c, w_out, w_up, w_down, final_norm):
    bsz, n_lat, _ = x.shape
    n_ctx = ctx.shape[1]
    depth = w_in.shape[0]
    assert n_ctx == TOKEN_TILE and n_ctx == M_CHUNK and n_lat % TOKEN_TILE == 0 and bsz + 1 <= 8
    nt = n_ctx + n_lat
    ntile = nt // TOKEN_TILE
    nch = nt // M_CHUNK

    xc = jnp.concatenate([ctx, x], axis=1).reshape(bsz * nt, D_MODEL)
    c_rows = jnp.zeros((8, D_MODEL), F32).at[:bsz].set(c).at[bsz].set(c_ctx)
    mods = _modulation(c_rows, w_ada, b_ada).reshape(depth, 8, N_MOD, D_MODEL)
    cos_t, sin_t = _rope_tables(n_ctx, n_lat)
    perm = _rope_perm()
    head_perm = lambda n: np.concatenate([h * A_HD + perm for h in range(n)])

    ya = None
    for layer in range(depth):
        need_ctx = layer < depth - 1
        tile_off = 0 if need_ctx else 1
        ml = mods[layer]
        modsel = jnp.stack([jnp.broadcast_to(ml[bsz], (bsz, N_MOD, D_MODEL)), ml[:bsz]], axis=1)
        wl = w_in[layer]
        o = COL_OFF
        w_m = jnp.concatenate([wl[:, o[0]:o[4]]], axis=1).astype(BF16)
        w_kt = wl[:, o[1]:o[2]].T.astype(BF16)
        w_g = wl[:, o[4]:o[5]]
        w_gp = jnp.zeros((D_MODEL, 2 * LANES), F32)
        w_gp = w_gp.at[:, 0:2 * M_HEADS].set(w_g[:, 0:2 * M_HEADS])
        w_gp = w_gp.at[:, LANES:LANES + 2 * M_HEADS].set(w_g[:, 2 * M_HEADS:]).astype(BF16)
        w_gt = w_g.T.astype(BF16)
        gb = mlstm_gate_b[layer].reshape(M_GATES)
        gbp = jnp.zeros((1, 2 * LANES), F32)
        gbp = gbp.at[0, 0:2 * M_HEADS].set(gb[0:2 * M_HEADS]).at[0, LANES:LANES + 2 * M_HEADS].set(gb[2 * M_HEADS:])
        gbt = gb.reshape(M_GATES, 1)
        w_a = jnp.concatenate([wl[:, o[5]:o[6]][:, head_perm(A_HEADS)], wl[:, o[6]:o[7]][:, head_perm(A_KV_HEADS)]],
                              axis=1).astype(BF16)
        w_vt = wl[:, o[7]:o[8]].T.astype(BF16)
        w_c = wl[:, o[8]:o[14]].astype(BF16)
        n1 = norm1[layer].reshape(1, D_MODEL)
        n2 = norm2[layer].reshape(1, D_MODEL)

        mq, mk, mkt, mv, mo, gates, gates_t = _inproj_mlstm(xc, modsel, n1, w_m, w_kt, w_gp, w_gt, gbp, gbt, ntile)
        aq, ak, avt = _inproj_attn(xc, modsel, n1, w_a, w_vt, q_norm[layer][perm].reshape(1, A_HD),
                                  k_norm[layer][perm].reshape(1, A_HD), cos_t, sin_t, ntile)
        cu, bg, sg = _inproj_conv(xc, modsel, n1, w_c, ntile)

        h2 = _mlstm(mq, mk, mkt, mv, gates, gates_t, bsz, nch)
        ya = _attention(aq, ak, avt, None, bsz, ntile, 1, ntile - 1, nt)
        if need_ctx:
            ya = _attention(aq, ak, avt, ya, bsz, ntile, 0, 1, n_ctx)

        xc = _merge(xc, modsel, h2, mo, mlstm_norm[layer].reshape(1, M_V), ya, cu, bg, conv_w[layer], sg,
                    w_proj_m[layer].astype(BF16), w_proj_a[layer].astype(BF16), w_proj_c[layer].astype(BF16),
                    w_out[layer].astype(BF16), bsz, ntile, tile_off)
        xc = _mlp(xc, modsel, n2, w_up[layer].astype(BF16), w_down[layer].astype(BF16), bsz, ntile, tile_off)

    return _final_norm(xc, final_norm.reshape(1, D_MODEL), bsz, ntile)
```

```python
import functools

import numpy as np
import jax
import jax.numpy as jnp
from jax import lax
from jax.experimental import pallas as pl
from jax.experimental.pallas import tpu as pltpu

D_MODEL = 1024
GRID_W = 64
M_HEADS = 4
M_DQK = 128
M_DV = 256
M_QK = M_HEADS * M_DQK
M_V = M_HEADS * M_DV
M_GATES = 2 * 2 * M_HEADS
A_HEADS = 8
A_KV_HEADS = 2
A_GROUP = A_HEADS // A_KV_HEADS
A_HD = 128
A_Q = A_HEADS * A_HD
A_KV = A_KV_HEADS * A_HD
ROPE_THETA = 10000.0
C_WIDTH = D_MODEL
D_FF = 4 * D_MODEL
N_MOD = 6
EPS = 1e-6
COL_SIZES = (M_QK, M_QK, M_V, M_V, M_GATES, A_Q, A_KV, A_KV, C_WIDTH, C_WIDTH, C_WIDTH, D_MODEL, D_MODEL, D_MODEL)
COL_OFF = tuple(int(v) for v in np.concatenate([[0], np.cumsum(COL_SIZES)]))

LANES = 128
SUBLANES = 8
BF16_SUBLANES = 16
TOKEN_TILE = 256
M_CHUNK = 256
ATTN_KV_TILE = 2816
VMEM_LIMIT = 56 * 1024 * 1024

F32 = jnp.float32
LOG2_E = 1.4426950408889634
BF16 = jnp.bfloat16
NT_DIMS = (((1,), (1,)), ((), ()))


def _params(sem):
    return pltpu.CompilerParams(dimension_semantics=sem, vmem_limit_bytes=VMEM_LIMIT)


def _const_spec(shape):
    nd = len(shape)
    return pl.BlockSpec(shape, lambda *_: (0,) * nd)


def _norm_mod(x, g, shift, scale):
    ms = jnp.mean(x * x, axis=-1, keepdims=True)
    y = x * lax.rsqrt(ms + EPS) * g
    return y * (1.0 + scale) + shift


def _sigmoid(x):
    return 1.0 / (1.0 + jnp.exp(-x))


def _log_sigmoid(x):
    return jnp.minimum(x, 0.0) - jnp.log(1.0 + jnp.exp(-jnp.abs(x)))


def _split3(x):
    hi = x.astype(BF16)
    r = x - hi.astype(F32)
    mid = r.astype(BF16)
    lo = (r - mid.astype(F32)).astype(BF16)
    return hi, mid, lo


def _mod_kernel(c_ref, w_ref, b_ref, o_ref):
    c = c_ref[...]
    s = c * _sigmoid(c)
    o_ref[...] = jnp.dot(s, w_ref[...], precision=lax.Precision.HIGHEST,
                         preferred_element_type=F32) + b_ref[...]


def _modulation(c_rows, w_ada, b_ada):
    depth = w_ada.shape[0]
    tn = 1536
    return pl.pallas_call(
        _mod_kernel,
        grid=(depth, N_MOD * D_MODEL // tn),
        in_specs=[
            pl.BlockSpec((8, D_MODEL), lambda l, j: (0, 0)),
            pl.BlockSpec((None, D_MODEL, tn), lambda l, j: (l, 0, j)),
            pl.BlockSpec((None, 1, tn), lambda l, j: (l, 0, j)),
        ],
        out_specs=pl.BlockSpec((None, 8, tn), lambda l, j: (l, 0, j)),
        out_shape=jax.ShapeDtypeStruct((depth, 8, N_MOD * D_MODEL), F32),
        compiler_params=_params(("arbitrary", "arbitrary")),
        name="modulation",
    )(c_rows, w_ada, b_ada.reshape(depth, 1, N_MOD * D_MODEL))


def _mod_spec(ntile):
    return pl.BlockSpec((None, None, N_MOD, D_MODEL), lambda i: (i // ntile, jnp.minimum(i % ntile, 1), 0, 0))


def _inproj_mlstm_kernel(x_ref, mod_ref, g_ref, w_ref, wt_ref, wg_ref, wgt_ref, gb_ref, gbt_ref,
                         k_ref, qt_ref, vt_ref, o_ref, gate_ref, gatet_ref):
    h = _norm_mod(x_ref[...], g_ref[...], mod_ref[0:1, :], mod_ref[1:2, :]).astype(BF16)
    k_ref[...] = jnp.dot(h, w_ref[:, 0:M_QK], preferred_element_type=F32).astype(BF16)
    o_ref[...] = jnp.dot(h, w_ref[:, M_QK:], preferred_element_type=F32).astype(BF16)
    qt = lax.dot_general(wt_ref[0:M_QK, :], h, NT_DIMS, preferred_element_type=F32)
    qt_ref[...] = (qt * (M_DQK ** -0.5)).astype(BF16)
    vt_ref[...] = lax.dot_general(wt_ref[M_QK:, :], h, NT_DIMS, preferred_element_type=F32).astype(BF16)
    ga = jnp.dot(h, wg_ref[...], preferred_element_type=F32) + gb_ref[...]
    lane = lax.broadcasted_iota(jnp.int32, ga.shape, 1) % LANES
    ga = jnp.where(lane >= M_HEADS, _log_sigmoid(ga), ga)
    gate_ref[0] = ga[:, :LANES]
    gate_ref[1] = ga[:, LANES:]
    gt = lax.dot_general(wgt_ref[...], h, NT_DIMS, preferred_element_type=F32) + gbt_ref[...]
    row = lax.broadcasted_iota(jnp.int32, gt.shape, 0) % (2 * M_HEADS)
    gt = jnp.where(row >= M_HEADS, _log_sigmoid(gt), gt)
    gatet_ref[0] = gt[:2 * M_HEADS]
    gatet_ref[1] = gt[2 * M_HEADS:]


def _inproj_mlstm(xc, modsel, norm_g, w, wt, wg, wgt, gb, gbt, ntile):
    t = xc.shape[0]
    tm = TOKEN_TILE
    tok = lambda n: pl.BlockSpec((tm, n), lambda i: (i, 0))
    feat = lambda n: pl.BlockSpec((n, tm), lambda i: (0, i))
    return pl.pallas_call(
        _inproj_mlstm_kernel,
        grid=(t // tm,),
        in_specs=[
            tok(D_MODEL), _mod_spec(ntile), _const_spec((1, D_MODEL)),
            _const_spec((D_MODEL, M_QK + M_V)),
            _const_spec((M_QK + M_V, D_MODEL)),
            _const_spec((D_MODEL, 2 * LANES)),
            _const_spec((M_GATES, D_MODEL)),
            _const_spec((1, 2 * LANES)),
            _const_spec((M_GATES, 1)),
        ],
        out_specs=[
            tok(M_QK), feat(M_QK), feat(M_V), tok(M_V),
            pl.BlockSpec((2, tm, LANES), lambda i: (0, i, 0)),
            pl.BlockSpec((2, 2 * M_HEADS, tm), lambda i: (0, 0, i)),
        ],
        out_shape=[
            jax.ShapeDtypeStruct((t, M_QK), BF16), jax.ShapeDtypeStruct((M_QK, t), BF16),
            jax.ShapeDtypeStruct((M_V, t), BF16), jax.ShapeDtypeStruct((t, M_V), BF16),
            jax.ShapeDtypeStruct((2, t, LANES), F32),
            jax.ShapeDtypeStruct((2, 2 * M_HEADS, t), F32),
        ],
        compiler_params=_params(("parallel",)),
        name="inproj_mlstm",
    )(xc, modsel, norm_g, w, wt, wg, wgt, gb, gbt)


def _head_norm_rope(xh, gain, cos, sin):
    ms = jnp.mean(xh * xh, axis=-1, keepdims=True)
    y = xh * lax.rsqrt(ms + EPS) * gain
    return y * cos + pltpu.roll(y, A_HD // 2, 1) * sin


def _inproj_attn_kernel(x_ref, mod_ref, g_ref, w_ref, wvt_ref, qg_ref, kg_ref, cos_ref, sin_ref,
                        q_ref, k_ref, vt_ref):
    h = _norm_mod(x_ref[...], g_ref[...], mod_ref[0:1, :], mod_ref[1:2, :]).astype(BF16)
    cos = cos_ref[...]
    sin = sin_ref[...]
    for pr in range((A_HEADS + A_KV_HEADS) // 2):
        two = jnp.dot(h, w_ref[:, pr * 2 * A_HD:(pr + 1) * 2 * A_HD], preferred_element_type=F32)
        for u in range(2):
            hd = 2 * pr + u
            xh = two[:, u * A_HD:(u + 1) * A_HD]
            if hd < A_HEADS:
                qh = _head_norm_rope(xh, qg_ref[...], cos, sin) * (A_HD ** -0.5 * LOG2_E)
                q_ref[:, hd * A_HD:(hd + 1) * A_HD] = qh.astype(BF16)
            else:
                kd = hd - A_HEADS
                k_ref[:, kd * A_HD:(kd + 1) * A_HD] = _head_norm_rope(xh, kg_ref[...], cos, sin).astype(BF16)
    vt_ref[...] = lax.dot_general(wvt_ref[...], h, NT_DIMS, preferred_element_type=F32).astype(BF16)


def _inproj_attn(xc, modsel, norm_g, w, wvt, qg, kg, cos, sin, ntile):
    t = xc.shape[0]
    tm = TOKEN_TILE
    tok = lambda n: pl.BlockSpec((tm, n), lambda i: (i, 0))
    return pl.pallas_call(
        _inproj_attn_kernel,
        grid=(t // tm,),
        in_specs=[
            tok(D_MODEL), _mod_spec(ntile), _const_spec((1, D_MODEL)),
            _const_spec((D_MODEL, A_Q + A_KV)),
            _const_spec((A_KV, D_MODEL)),
            _const_spec((1, A_HD)), _const_spec((1, A_HD)),
            pl.BlockSpec((tm, A_HD), lambda i: (i % ntile, 0)),
            pl.BlockSpec((tm, A_HD), lambda i: (i % ntile, 0)),
        ],
        out_specs=[tok(A_Q), tok(A_KV), pl.BlockSpec((None, A_KV, tm), lambda i: (i, 0, 0))],
        out_shape=[jax.ShapeDtypeStruct((t, A_Q), BF16), jax.ShapeDtypeStruct((t, A_KV), BF16),
                   jax.ShapeDtypeStruct((t // tm, A_KV, tm), BF16)],
        compiler_params=_params(("parallel",)),
        name="inproj_attn",
    )(xc, modsel, norm_g, w, wvt, qg, kg, cos, sin)


def _inproj_conv_kernel(x_ref, mod_ref, g_ref, w_ref, cu_ref, b_ref, sg_ref):
    h = _norm_mod(x_ref[...], g_ref[...], mod_ref[0:1, :], mod_ref[1:2, :]).astype(BF16)
    w = C_WIDTH
    u = jnp.dot(h, w_ref[:, 0:w], preferred_element_type=F32)
    cg = jnp.dot(h, w_ref[:, 2 * w:3 * w], preferred_element_type=F32)
    cu_ref[...] = (cg * u).astype(BF16)
    b_ref[...] = jnp.dot(h, w_ref[:, w:2 * w], preferred_element_type=F32).astype(BF16)
    for j in range(3):
        gj = jnp.dot(h, w_ref[:, (3 + j) * w:(4 + j) * w], preferred_element_type=F32)
        sg_ref[:, j * w:(j + 1) * w] = _sigmoid(gj).astype(BF16)


def _inproj_conv(xc, modsel, norm_g, w, ntile):
    t = xc.shape[0]
    tm = TOKEN_TILE
    tok = lambda n: pl.BlockSpec((tm, n), lambda i: (i, 0))
    return pl.pallas_call(
        _inproj_conv_kernel,
        grid=(t // tm,),
        in_specs=[tok(D_MODEL), _mod_spec(ntile), _const_spec((1, D_MODEL)), _const_spec((D_MODEL, 6 * C_WIDTH))],
        out_specs=[tok(C_WIDTH), tok(C_WIDTH), tok(3 * D_MODEL)],
        out_shape=[jax.ShapeDtypeStruct((t, C_WIDTH), BF16), jax.ShapeDtypeStruct((t, C_WIDTH), BF16),
                   jax.ShapeDtypeStruct((t, 3 * D_MODEL), BF16)],
        compiler_params=_params(("parallel",)),
        name="inproj_conv",
    )(xc, modsel, norm_g, w)


def _mlstm_kernel(k_ref, qt_ref, vt_ref, g_ref, gt_ref, h_ref, ct_ref, nt_ref, m_ref):
    d = pl.program_id(0)
    j = pl.program_id(2)
    L = M_CHUNK

    @pl.when(j == 0)
    def _():
        ct_ref[...] = jnp.zeros_like(ct_ref)
        nt_ref[...] = jnp.zeros_like(nt_ref)
        m_ref[...] = jnp.zeros_like(m_ref)

    sign = 1 - 2 * d
    row = lax.broadcasted_iota(jnp.int32, (L, L), 0)
    col = lax.broadcasted_iota(jnp.int32, (L, L), 1)
    mask = (col - row) * sign >= 0
    mask_b = jnp.where(mask, 1.0, 0.0).astype(BF16)
    maskt_b = jnp.where((row - col) * sign >= 0, 1.0, 0.0).astype(BF16)
    g = g_ref[...]
    gt = gt_ref[...]
    cum_col = sum(jnp.dot(maskt_b, p, preferred_element_type=F32) for p in _split3(g))
    cum_row = sum(jnp.dot(p, mask_b, preferred_element_type=F32) for p in _split3(gt))
    tot = jnp.sum(gt, axis=-1, keepdims=True)
    src = cum_col - pltpu.roll(g, M_HEADS, 1)

    for hd in range(M_HEADS):
        fi = M_HEADS + hd
        bt = cum_row[fi:fi + 1, :]
        it = gt[hd:hd + 1, :]
        total = tot[fi:fi + 1, :]
        m_old = m_ref[hd]
        dmat = jnp.where(mask, bt - src[:, fi:fi + 1], -jnp.inf)
        inter = bt + m_old
        m_t = jnp.maximum(inter, jnp.max(dmat, axis=0, keepdims=True))
        w_inter = jnp.exp(inter - m_t)
        kh = k_ref[:, hd * M_DQK:(hd + 1) * M_DQK]
        qth = qt_ref[hd * M_DQK:(hd + 1) * M_DQK, :]
        vth = vt_ref[hd * M_DV:(hd + 1) * M_DV, :]
        s = jnp.dot(kh, qth, preferred_element_type=F32) * jnp.exp(dmat - m_t)
        ct_old = ct_ref[hd]
        nt_old = nt_ref[hd]
        num = (jnp.dot(vth, s.astype(BF16), preferred_element_type=F32)
               + w_inter * jnp.dot(ct_old.astype(BF16), qth, preferred_element_type=F32))
        n_hi = nt_old.astype(BF16)
        n_lo = (nt_old - n_hi.astype(F32)).astype(BF16)
        qn = (jnp.dot(n_hi, qth, preferred_element_type=F32)
              + jnp.dot(n_lo, qth, preferred_element_type=F32))[0:1, :]
        den = jnp.sum(s, axis=0, keepdims=True) + w_inter * qn
        inv = 1.0 / jnp.maximum(jnp.abs(den), jnp.exp(-m_t))
        h_ref[:, hd * M_DV:(hd + 1) * M_DV] = (num * inv).T.astype(BF16)

        grow = total - bt + it
        m_new = jnp.maximum(total + m_old[:, 0:1], jnp.max(grow, axis=-1, keepdims=True))
        w_g = jnp.exp(grow - m_new)
        w_c = jnp.exp(total + m_old[:, 0:1] - m_new)
        vw = (vth.astype(F32) * w_g).astype(BF16)
        ct_ref[hd] = w_c * ct_old + jnp.dot(vw, kh, preferred_element_type=F32)
        wg8 = jnp.broadcast_to(w_g, (SUBLANES, L)).astype(BF16)
        nt_ref[hd] = w_c * nt_old + jnp.dot(wg8, kh, preferred_element_type=F32)
        m_ref[hd] = jnp.broadcast_to(m_new, (1, L))


def _mlstm(k, qt, vt, gates, gates_t, bsz, nch):
    t = k.shape[0]
    L = M_CHUNK

    def chunk(d, b, j):
        return b * nch + jnp.where(d == 0, j, jnp.where(j == 0, 0, nch - j))

    return pl.pallas_call(
        _mlstm_kernel,
        grid=(2, bsz, nch),
        in_specs=[
            pl.BlockSpec((L, M_QK), lambda d, b, j: (chunk(d, b, j), 0)),
            pl.BlockSpec((M_QK, L), lambda d, b, j: (0, chunk(d, b, j))),
            pl.BlockSpec((M_V, L), lambda d, b, j: (0, chunk(d, b, j))),
            pl.BlockSpec((None, L, LANES), lambda d, b, j: (d, chunk(d, b, j), 0)),
            pl.BlockSpec((None, 2 * M_HEADS, L), lambda d, b, j: (d, 0, chunk(d, b, j))),
        ],
        out_specs=pl.BlockSpec((None, L, M_V), lambda d, b, j: (d, chunk(d, b, j), 0)),
        out_shape=jax.ShapeDtypeStruct((2, t, M_V), BF16),
        scratch_shapes=[pltpu.VMEM((M_HEADS, M_DV, M_DQK), F32), pltpu.VMEM((M_HEADS, SUBLANES, M_DQK), F32),
                        pltpu.VMEM((M_HEADS, 1, L), F32)],
        compiler_params=_params(("arbitrary", "arbitrary", "arbitrary")),
        name="mlstm_scan",
    )(k, qt, vt, gates, gates_t)


def _attn_kernel(q_ref, k_ref, vt_ref, o_ref, acc_ref, s_ref, *, nkt, sub):
    tq = q_ref.shape[0]
    tt = vt_ref.shape[2]
    tk = sub * tt
    qs = [q_ref[:, g * A_HD:(g + 1) * A_HD] for g in range(A_GROUP)]
    acc_ref[...] = jnp.zeros_like(acc_ref)

    def scores(c, g, j, slot):
        kj = k_ref[pl.ds(pl.multiple_of(c * tk + j * tt, tt), tt), :]
        s = lax.dot_general(kj, qs[g], NT_DIMS, preferred_element_type=F32)
        s_ref[slot, j * tt:(j + 1) * tt, :] = s
        return jnp.max(s.reshape(tt // SUBLANES, SUBLANES, tq), axis=0)

    def colmax(cm8):
        return jnp.max(cm8, axis=0, keepdims=True)

    cm8 = scores(0, 0, 0, 0)
    for j in range(1, sub):
        cm8 = jnp.maximum(cm8, scores(0, 0, j, 0))

    def body(c, carry):
        ms, ls, cm = list(carry[0]), list(carry[1]), carry[2]
        c_next = jnp.minimum(c + 1, nkt - 1)
        for g in range(A_GROUP):
            cn, gn = (c, g + 1) if g + 1 < A_GROUP else (c_next, 0)
            m_new = jnp.maximum(ms[g], cm)
            alpha = jnp.exp2(ms[g] - m_new)
            l8 = None
            pv = None
            cm8n = None
            for j in range(sub):
                c8 = scores(cn, gn, j, (g + 1) % 2)
                cm8n = c8 if cm8n is None else jnp.maximum(cm8n, c8)
                p = jnp.exp2(s_ref[g % 2, j * tt:(j + 1) * tt, :] - m_new)
                p8 = jnp.sum(p.reshape(tt // SUBLANES, SUBLANES, tq), axis=0)
                l8 = p8 if l8 is None else l8 + p8
                d = jnp.dot(vt_ref[c * sub + j], p.astype(BF16), preferred_element_type=F32)
                pv = d if pv is None else pv + d
            acc_ref[g] = alpha * acc_ref[g] + pv
            ls[g] = alpha * ls[g] + l8
            ms[g] = m_new
            cm = colmax(cm8n)
        return tuple(ms), tuple(ls), cm

    init = (tuple(jnp.full((1, tq), -jnp.inf, F32) for _ in range(A_GROUP)),
            tuple(jnp.zeros((SUBLANES, tq), F32) for _ in range(A_GROUP)), colmax(cm8))
    _, ls, _ = lax.fori_loop(0, nkt, body, init)
    for g in range(A_GROUP):
        l = jnp.sum(ls[g], axis=0, keepdims=True)
        o_ref[:, g * A_HD:(g + 1) * A_HD] = (acc_ref[g] / l).T.astype(BF16)


def _attention(aq, ak, avt, bsz, ntile, q_off, nq, nk):
    tq = TOKEN_TILE
    tt = avt.shape[2]
    sub = min(ATTN_KV_TILE, nk) // tt
    tk = sub * tt
    nt = ntile * tq
    assert nt % nk == 0 and nk % tk == 0
    return pl.pallas_call(
        functools.partial(_attn_kernel, nkt=nk // tk, sub=sub),
        grid=(bsz, A_KV_HEADS, nq),
        in_specs=[
            pl.BlockSpec((tq, A_GROUP * A_HD), lambda b, h, i: (b * ntile + q_off + i, h)),
            pl.BlockSpec((nk, A_HD), lambda b, h, i: (b * (nt // nk), h)),
            pl.BlockSpec((nk // tt, A_HD, tt), lambda b, h, i: (b * (nt // nk), h, 0)),
        ],
        out_specs=pl.BlockSpec((tq, A_GROUP * A_HD), lambda b, h, i: (b * nq + i, h)),
        out_shape=jax.ShapeDtypeStruct((bsz * nq * tq, A_Q), BF16),
        scratch_shapes=[pltpu.VMEM((A_GROUP, A_HD, tq), F32), pltpu.VMEM((2, tk, tq), F32)],
        compiler_params=_params(("parallel", "parallel", "arbitrary")),
        name="gqa_attention",
    )(aq, ak, avt)


def _merge_kernel(x_ref, mod_ref, hf_ref, hr_ref, o_ref, mg_ref, yac_ref, yal_ref, cu_ref, cup_ref, cun_ref, b_ref,
                  cw_ref, sg_ref, wpm_ref, wpa_ref, wpc_ref, wo_ref, out_ref, *, ntile, tile_off):
    pos = pl.program_id(1) + tile_off
    tm = x_ref.shape[0]
    hsum = hf_ref[...].astype(F32) + hr_ref[...].astype(F32)
    parts = []
    for hd in range(M_HEADS):
        hh = hsum[:, hd * M_DV:(hd + 1) * M_DV]
        parts.append(hh * lax.rsqrt(jnp.mean(hh * hh, axis=-1, keepdims=True) + EPS))
    hn = jnp.concatenate(parts, axis=-1) * mg_ref[...]
    ym = (hn * _sigmoid(o_ref[...].astype(F32))).astype(BF16)
    ya = jnp.where(pos == 0, yac_ref[...], yal_ref[...])
    has_prev = pos >= 2
    has_next = jnp.logical_and(pos >= 1, pos <= ntile - 2)
    cu = cu_ref[...].astype(F32)
    prev_row = jnp.where(has_prev, cup_ref[BF16_SUBLANES - 1:BF16_SUBLANES, :].astype(F32), 0.0)
    next_row = jnp.where(has_next, cun_ref[0:1, :].astype(F32), 0.0)
    ridx = lax.broadcasted_iota(jnp.int32, cu.shape, 0)
    up = jnp.where(ridx == 0, prev_row, pltpu.roll(cu, 1, 0))
    dn = jnp.where(ridx == tm - 1, next_row, pltpu.roll(cu, tm - 1, 0))
    conv = cw_ref[0:1, :] * up + cw_ref[1:2, :] * cu + cw_ref[2:3, :] * dn
    yc = (b_ref[...].astype(F32) * conv).astype(BF16)
    w = D_MODEL
    merged = (sg_ref[:, 0:w].astype(F32) * jnp.dot(ym, wpm_ref[...], preferred_element_type=F32)
              + sg_ref[:, w:2 * w].astype(F32) * jnp.dot(ya, wpa_ref[...], preferred_element_type=F32)
              + sg_ref[:, 2 * w:3 * w].astype(F32) * jnp.dot(yc, wpc_ref[...], preferred_element_type=F32))
    out = jnp.dot(merged.astype(BF16), wo_ref[...], preferred_element_type=F32)
    out_ref[...] = x_ref[...] + mod_ref[2:3, :] * out


def _merge(xc, modsel, h2, o, mgain, ya_ctx, ya_lat, cu, bg, conv_w, sg, wpm, wpa, wpc, wo, bsz, ntile, tile_off):
    t = xc.shape[0]
    tm = TOKEN_TILE
    hb = tm // BF16_SUBLANES
    nhb = t // BF16_SUBLANES
    nout = ntile - tile_off
    tile = lambda b, i: b * ntile + tile_off + i
    tok = lambda n: pl.BlockSpec((tm, n), lambda b, i: (tile(b, i), 0))
    return pl.pallas_call(
        functools.partial(_merge_kernel, ntile=ntile, tile_off=tile_off),
        grid=(bsz, nout),
        in_specs=[
            tok(D_MODEL),
            pl.BlockSpec((None, None, N_MOD, D_MODEL), lambda b, i: (b, jnp.minimum(i + tile_off, 1), 0, 0)),
            pl.BlockSpec((None, tm, M_V), lambda b, i: (0, tile(b, i), 0)),
            pl.BlockSpec((None, tm, M_V), lambda b, i: (1, tile(b, i), 0)),
            tok(M_V),
            _const_spec((1, M_V)),
            pl.BlockSpec((tm, A_Q), lambda b, i: (b, 0)),
            pl.BlockSpec((tm, A_Q), lambda b, i: (b * (ntile - 1) + jnp.maximum(i + tile_off - 1, 0), 0)),
            tok(C_WIDTH),
            pl.BlockSpec((BF16_SUBLANES, C_WIDTH), lambda b, i: (jnp.maximum(tile(b, i) * hb - 1, 0), 0)),
            pl.BlockSpec((BF16_SUBLANES, C_WIDTH), lambda b, i: (jnp.minimum((tile(b, i) + 1) * hb, nhb - 1), 0)),
            tok(C_WIDTH),
            _const_spec((3, C_WIDTH)),
            tok(3 * D_MODEL),
            _const_spec((M_V, D_MODEL)), _const_spec((A_Q, D_MODEL)), _const_spec((C_WIDTH, D_MODEL)),
            _const_spec((D_MODEL, D_MODEL)),
        ],
        out_specs=pl.BlockSpec((tm, D_MODEL), lambda b, i: (b * nout + i, 0)),
        out_shape=jax.ShapeDtypeStruct((bsz * nout * tm, D_MODEL), F32),
        compiler_params=_params(("parallel", "parallel")),
        name="merge_branches",
    )(xc, modsel, h2, h2, o, mgain, ya_ctx, ya_lat, cu, cu, cu, bg, conv_w, sg, wpm, wpa, wpc, wo)


def _mlp_kernel(x_ref, mod_ref, g_ref, wup_ref, wdn_ref, fg_ref, out_ref, *, final):
    x = x_ref[...]
    h = _norm_mod(x, g_ref[...], mod_ref[3:4, :], mod_ref[4:5, :]).astype(BF16)
    acc = jnp.zeros(x.shape, F32)
    fc = D_FF // 4
    for c in range(4):
        a = jnp.dot(h, wup_ref[:, c * fc:(c + 1) * fc], preferred_element_type=F32)
        a = jnp.square(jnp.maximum(a, 0.0)).astype(BF16)
        acc = acc + jnp.dot(a, wdn_ref[c * fc:(c + 1) * fc, :], preferred_element_type=F32)
    y = x + mod_ref[5:6, :] * acc
    if final:
        ms = jnp.mean(y * y, axis=-1, keepdims=True)
        y = y * lax.rsqrt(ms + EPS) * fg_ref[...]
    out_ref[...] = y


def _mlp(xs, modsel, norm_g, wup, wdn, final_g, bsz, ntile_x, ctx_first, final):
    tm = TOKEN_TILE
    tok = pl.BlockSpec((tm, D_MODEL), lambda b, i: (b * ntile_x + i, 0))
    return pl.pallas_call(
        functools.partial(_mlp_kernel, final=final),
        grid=(bsz, ntile_x),
        in_specs=[
            tok,
            pl.BlockSpec((None, None, N_MOD, D_MODEL),
                         lambda b, i: (b, jnp.minimum(i, 1) if ctx_first else 1, 0, 0)),
            _const_spec((1, D_MODEL)),
            _const_spec((D_MODEL, D_FF)),
            _const_spec((D_FF, D_MODEL)),
            _const_spec((1, D_MODEL)),
        ],
        out_specs=tok,
        out_shape=jax.ShapeDtypeStruct(xs.shape, F32),
        compiler_params=_params(("parallel", "parallel")),
        name="relu2_mlp",
    )(xs, modsel, norm_g, wup, wdn, final_g)


def _rope_perm():
    p = np.arange(A_HD)
    part, axis, f = p // 64, (p // 32) % 2, p % 32
    return axis * 64 + part * 32 + f


def _rope_tables(n_ctx, n_lat):
    rows = n_lat // GRID_W
    row_idx = jnp.repeat(jnp.arange(rows), GRID_W)
    col_idx = jnp.tile(jnp.arange(GRID_W), rows)
    n_freq = A_HD // 4
    inv_freq = ROPE_THETA ** (-jnp.arange(n_freq, dtype=F32) / n_freq)
    pos = jnp.stack([row_idx, col_idx], axis=-1).astype(F32)
    ang = pos[:, :, None] * inv_freq
    cos = jnp.cos(ang).reshape(n_lat, 2 * n_freq)
    sin = jnp.sin(ang).reshape(n_lat, 2 * n_freq)
    cos_t = jnp.concatenate([cos, cos], axis=-1)
    sin_t = jnp.concatenate([-sin, sin], axis=-1)
    cos_t = jnp.concatenate([jnp.ones((n_ctx, A_HD), F32), cos_t], axis=0)
    sin_t = jnp.concatenate([jnp.zeros((n_ctx, A_HD), F32), sin_t], axis=0)
    return cos_t, sin_t


def kernel(x, c, ctx, c_ctx, w_ada, b_ada, norm1, norm2, w_in, mlstm_gate_b, mlstm_norm, q_norm, k_norm,
           conv_w, w_proj_m, w_proj_a, w_proj_c, w_out, w_up, w_down, final_norm):
    bsz, n_lat, _ = x.shape
    n_ctx = ctx.shape[1]
    depth = w_in.shape[0]
    assert n_ctx == TOKEN_TILE and n_ctx == M_CHUNK and n_lat % TOKEN_TILE == 0 and bsz + 1 <= 8
    nt = n_ctx + n_lat
    ntile = nt // TOKEN_TILE
    nch = nt // M_CHUNK

    xc = jnp.concatenate([ctx, x], axis=1).reshape(bsz * nt, D_MODEL)
    c_rows = jnp.zeros((8, D_MODEL), F32).at[:bsz].set(c).at[bsz].set(c_ctx)
    mods = _modulation(c_rows, w_ada, b_ada).reshape(depth, 8, N_MOD, D_MODEL)
    cos_t, sin_t = _rope_tables(n_ctx, n_lat)
    perm = _rope_perm()
    head_perm = lambda n: np.concatenate([h * A_HD + perm for h in range(n)])
    fg = final_norm.reshape(1, D_MODEL)

    for layer in range(depth):
        need_ctx = layer < depth - 1
        tile_off = 0 if need_ctx else 1
        ml = mods[layer]
        modsel = jnp.stack([jnp.broadcast_to(ml[bsz], (bsz, N_MOD, D_MODEL)), ml[:bsz]], axis=1)
        wl = w_in[layer]
        o = COL_OFF
        w_m = jnp.concatenate([wl[:, o[1]:o[2]], wl[:, o[3]:o[4]]], axis=1).astype(BF16)
        w_mt = jnp.concatenate([wl[:, o[0]:o[1]], wl[:, o[2]:o[3]]], axis=1).T.astype(BF16)
        w_g = wl[:, o[4]:o[5]]
        w_gp = jnp.zeros((D_MODEL, 2 * LANES), F32)
        w_gp = w_gp.at[:, 0:2 * M_HEADS].set(w_g[:, 0:2 * M_HEADS])
        w_gp = w_gp.at[:, LANES:LANES + 2 * M_HEADS].set(w_g[:, 2 * M_HEADS:]).astype(BF16)
        w_gt = w_g.T.astype(BF16)
        gb = mlstm_gate_b[layer].reshape(M_GATES)
        gbp = jnp.zeros((1, 2 * LANES), F32)
        gbp = gbp.at[0, 0:2 * M_HEADS].set(gb[0:2 * M_HEADS]).at[0, LANES:LANES + 2 * M_HEADS].set(gb[2 * M_HEADS:])
        gbt = gb.reshape(M_GATES, 1)
        w_a = jnp.concatenate([wl[:, o[5]:o[6]][:, head_perm(A_HEADS)], wl[:, o[6]:o[7]][:, head_perm(A_KV_HEADS)]],
                              axis=1).astype(BF16)
        w_vt = wl[:, o[7]:o[8]].T.astype(BF16)
        w_c = wl[:, o[8]:o[14]].astype(BF16)
        n1 = norm1[layer].reshape(1, D_MODEL)
        n2 = norm2[layer].reshape(1, D_MODEL)

        mk, mqt, mvt, mo, gates, gates_t = _inproj_mlstm(xc, modsel, n1, w_m, w_mt, w_gp, w_gt, gbp, gbt, ntile)
        aq, ak, avt = _inproj_attn(xc, modsel, n1, w_a, w_vt, q_norm[layer][perm].reshape(1, A_HD),
                                   k_norm[layer][perm].reshape(1, A_HD), cos_t, sin_t, ntile)
        cu, bg, sg = _inproj_conv(xc, modsel, n1, w_c, ntile)

        h2 = _mlstm(mk, mqt, mvt, gates, gates_t, bsz, nch)
        ya_lat = _attention(aq, ak, avt, bsz, ntile, 1, ntile - 1, nt)
        ya_ctx = _attention(aq, ak, avt, bsz, ntile, 0, 1, n_ctx) if need_ctx else ya_lat

        xs = _merge(xc, modsel, h2, mo, mlstm_norm[layer].reshape(1, M_V), ya_ctx, ya_lat, cu, bg, conv_w[layer], sg,
                    w_proj_m[layer].astype(BF16), w_proj_a[layer].astype(BF16), w_proj_c[layer].astype(BF16),
                    w_out[layer].astype(BF16), bsz, ntile, tile_off)
        xc = _mlp(xs, modsel, n2, w_up[layer].astype(BF16), w_down[layer].astype(BF16), fg, bsz,
                  ntile - tile_off, need_ctx, not need_ctx)

    return xc.reshape(bsz, n_lat, D_MODEL)
```

```python
import functools

import numpy as np
import jax
import jax.numpy as jnp
from jax import lax
from jax.experimental import pallas as pl
from jax.experimental.pallas import tpu as pltpu

D_MODEL = 1024
GRID_W = 64
M_HEADS = 4
M_DQK = 128
M_DV = 256
M_QK = M_HEADS * M_DQK
M_V = M_HEADS * M_DV
M_GATES = 2 * 2 * M_HEADS
A_HEADS = 8
A_KV_HEADS = 2
A_GROUP = A_HEADS // A_KV_HEADS
A_HD = 128
A_Q = A_HEADS * A_HD
A_KV = A_KV_HEADS * A_HD
ROPE_THETA = 10000.0
C_WIDTH = D_MODEL
D_FF = 4 * D_MODEL
N_MOD = 6
EPS = 1e-6
COL_SIZES = (M_QK, M_QK, M_V, M_V, M_GATES, A_Q, A_KV, A_KV, C_WIDTH, C_WIDTH, C_WIDTH, D_MODEL, D_MODEL, D_MODEL)
COL_OFF = tuple(int(v) for v in np.concatenate([[0], np.cumsum(COL_SIZES)]))

LANES = 128
SUBLANES = 8
BF16_SUBLANES = 16
TOKEN_TILE = 256
M_CHUNK = 256
ATTN_KV_TILE = 2816
VMEM_LIMIT = 56 * 1024 * 1024

F32 = jnp.float32
LOG2_E = 1.4426950408889634
BF16 = jnp.bfloat16
NT_DIMS = (((1,), (1,)), ((), ()))


def _params(sem):
    return pltpu.CompilerParams(dimension_semantics=sem, vmem_limit_bytes=VMEM_LIMIT)


def _const_spec(shape):
    nd = len(shape)
    return pl.BlockSpec(shape, lambda *_: (0,) * nd, pipeline_mode=pl.Buffered(1))


def _norm_mod(x, g, shift, scale):
    ms = jnp.mean(x * x, axis=-1, keepdims=True)
    y = x * lax.rsqrt(ms + EPS) * g
    return y * (1.0 + scale) + shift


def _sigmoid(x):
    return 1.0 / (1.0 + jnp.exp(-x))


def _log_sigmoid(x):
    return jnp.minimum(x, 0.0) - jnp.log(1.0 + jnp.exp(-jnp.abs(x)))


def _split3(x):
    hi = x.astype(BF16)
    r = x - hi.astype(F32)
    mid = r.astype(BF16)
    lo = (r - mid.astype(F32)).astype(BF16)
    return hi, mid, lo


def _mod_kernel(c_ref, w_ref, b_ref, o_ref):
    c = c_ref[...]
    s = c * _sigmoid(c)
    o_ref[...] = jnp.dot(s, w_ref[...], precision=lax.Precision.HIGHEST,
                         preferred_element_type=F32) + b_ref[...]


def _modulation(c_rows, w_ada, b_ada):
    depth = w_ada.shape[0]
    tn = 1536
    return pl.pallas_call(
        _mod_kernel,
        grid=(depth, N_MOD * D_MODEL // tn),
        in_specs=[
            pl.BlockSpec((8, D_MODEL), lambda l, j: (0, 0)),
            pl.BlockSpec((None, D_MODEL, tn), lambda l, j: (l, 0, j)),
            pl.BlockSpec((None, 1, tn), lambda l, j: (l, 0, j)),
        ],
        out_specs=pl.BlockSpec((None, 8, tn), lambda l, j: (l, 0, j)),
        out_shape=jax.ShapeDtypeStruct((depth, 8, N_MOD * D_MODEL), F32),
        compiler_params=_params(("arbitrary", "arbitrary")),
        name="modulation",
    )(c_rows, w_ada, b_ada.reshape(depth, 1, N_MOD * D_MODEL))


def _mod_spec(ntile):
    return pl.BlockSpec((None, None, N_MOD, D_MODEL), lambda i: (i // ntile, jnp.minimum(i % ntile, 1), 0, 0))


def _inproj_mlstm_kernel(x_ref, mod_ref, g_ref, w_ref, wt_ref, wg_ref, wgt_ref, gb_ref, gbt_ref,
                         k_ref, qt_ref, vt_ref, o_ref, gate_ref, gatet_ref):
    h = _norm_mod(x_ref[...], g_ref[...], mod_ref[0:1, :], mod_ref[1:2, :]).astype(BF16)
    k_ref[...] = jnp.dot(h, w_ref[:, 0:M_QK], preferred_element_type=F32).astype(BF16)
    o_ref[...] = jnp.dot(h, w_ref[:, M_QK:], preferred_element_type=F32).astype(BF16)
    qt = lax.dot_general(wt_ref[0:M_QK, :], h, NT_DIMS, preferred_element_type=F32)
    qt_ref[...] = (qt * (M_DQK ** -0.5)).astype(BF16)
    vt_ref[...] = lax.dot_general(wt_ref[M_QK:, :], h, NT_DIMS, preferred_element_type=F32).astype(BF16)
    ga = jnp.dot(h, wg_ref[...], preferred_element_type=F32) + gb_ref[...]
    lane = lax.broadcasted_iota(jnp.int32, ga.shape, 1) % LANES
    ga = jnp.where(lane >= M_HEADS, _log_sigmoid(ga), ga)
    gate_ref[0] = ga[:, :LANES]
    gate_ref[1] = ga[:, LANES:]
    gt = lax.dot_general(wgt_ref[...], h, NT_DIMS, preferred_element_type=F32) + gbt_ref[...]
    row = lax.broadcasted_iota(jnp.int32, gt.shape, 0) % (2 * M_HEADS)
    gt = jnp.where(row >= M_HEADS, _log_sigmoid(gt), gt)
    gatet_ref[0] = gt[:2 * M_HEADS]
    gatet_ref[1] = gt[2 * M_HEADS:]


def _inproj_mlstm(xc, modsel, norm_g, w, wt, wg, wgt, gb, gbt, ntile):
    t = xc.shape[0]
    tm = TOKEN_TILE
    tok = lambda n: pl.BlockSpec((tm, n), lambda i: (i, 0))
    feat = lambda n: pl.BlockSpec((n, tm), lambda i: (0, i))
    return pl.pallas_call(
        _inproj_mlstm_kernel,
        grid=(t // tm,),
        in_specs=[
            tok(D_MODEL), _mod_spec(ntile), _const_spec((1, D_MODEL)),
            _const_spec((D_MODEL, M_QK + M_V)),
            _const_spec((M_QK + M_V, D_MODEL)),
            _const_spec((D_MODEL, 2 * LANES)),
            _const_spec((M_GATES, D_MODEL)),
            _const_spec((1, 2 * LANES)),
            _const_spec((M_GATES, 1)),
        ],
        out_specs=[
            tok(M_QK), feat(M_QK), feat(M_V), tok(M_V),
            pl.BlockSpec((2, tm, LANES), lambda i: (0, i, 0)),
            pl.BlockSpec((2, 2 * M_HEADS, tm), lambda i: (0, 0, i)),
        ],
        out_shape=[
            jax.ShapeDtypeStruct((t, M_QK), BF16), jax.ShapeDtypeStruct((M_QK, t), BF16),
            jax.ShapeDtypeStruct((M_V, t), BF16), jax.ShapeDtypeStruct((t, M_V), BF16),
            jax.ShapeDtypeStruct((2, t, LANES), F32),
            jax.ShapeDtypeStruct((2, 2 * M_HEADS, t), F32),
        ],
        compiler_params=_params(("parallel",)),
        name="inproj_mlstm",
    )(xc, modsel, norm_g, w, wt, wg, wgt, gb, gbt)


def _head_norm_rope(xh, gain, cos, sin):
    ms = jnp.mean(xh * xh, axis=-1, keepdims=True)
    y = xh * lax.rsqrt(ms + EPS) * gain
    return y * cos + pltpu.roll(y, A_HD // 2, 1) * sin


def _inproj_attn_kernel(x_ref, mod_ref, g_ref, w_ref, wvt_ref, qg_ref, kg_ref, cos_ref, sin_ref,
                        q_ref, k_ref, vt_ref):
    h = _norm_mod(x_ref[...], g_ref[...], mod_ref[0:1, :], mod_ref[1:2, :]).astype(BF16)
    cos = cos_ref[...]
    sin = sin_ref[...]
    for pr in range((A_HEADS + A_KV_HEADS) // 2):
        two = jnp.dot(h, w_ref[:, pr * 2 * A_HD:(pr + 1) * 2 * A_HD], preferred_element_type=F32)
        for u in range(2):
            hd = 2 * pr + u
            xh = two[:, u * A_HD:(u + 1) * A_HD]
            if hd < A_HEADS:
                qh = _head_norm_rope(xh, qg_ref[...], cos, sin) * (A_HD ** -0.5 * LOG2_E)
                q_ref[:, hd * A_HD:(hd + 1) * A_HD] = qh.astype(BF16)
            else:
                kd = hd - A_HEADS
                k_ref[:, kd * A_HD:(kd + 1) * A_HD] = _head_norm_rope(xh, kg_ref[...], cos, sin).astype(BF16)
    vt_ref[...] = lax.dot_general(wvt_ref[...], h, NT_DIMS, preferred_element_type=F32).astype(BF16)


def _inproj_attn(xc, modsel, norm_g, w, wvt, qg, kg, cos, sin, ntile):
    t = xc.shape[0]
    tm = TOKEN_TILE
    tok = lambda n: pl.BlockSpec((tm, n), lambda i: (i, 0))
    return pl.pallas_call(
        _inproj_attn_kernel,
        grid=(t // tm,),
        in_specs=[
            tok(D_MODEL), _mod_spec(ntile), _const_spec((1, D_MODEL)),
            _const_spec((D_MODEL, A_Q + A_KV)),
            _const_spec((A_KV, D_MODEL)),
            _const_spec((1, A_HD)), _const_spec((1, A_HD)),
            pl.BlockSpec((tm, A_HD), lambda i: (i % ntile, 0)),
            pl.BlockSpec((tm, A_HD), lambda i: (i % ntile, 0)),
        ],
        out_specs=[tok(A_Q), tok(A_KV), pl.BlockSpec((None, A_KV, tm), lambda i: (i, 0, 0))],
        out_shape=[jax.ShapeDtypeStruct((t, A_Q), BF16), jax.ShapeDtypeStruct((t, A_KV), BF16),
                   jax.ShapeDtypeStruct((t // tm, A_KV, tm), BF16)],
        compiler_params=_params(("parallel",)),
        name="inproj_attn",
    )(xc, modsel, norm_g, w, wvt, qg, kg, cos, sin)


def _inproj_conv_kernel(x_ref, mod_ref, g_ref, w_ref, cu_ref, b_ref, sg_ref):
    h = _norm_mod(x_ref[...], g_ref[...], mod_ref[0:1, :], mod_ref[1:2, :]).astype(BF16)
    w = C_WIDTH
    u = jnp.dot(h, w_ref[:, 0:w], preferred_element_type=F32)
    cg = jnp.dot(h, w_ref[:, 2 * w:3 * w], preferred_element_type=F32)
    cu_ref[...] = (cg * u).astype(BF16)
    b_ref[...] = jnp.dot(h, w_ref[:, w:2 * w], preferred_element_type=F32).astype(BF16)
    for j in range(3):
        gj = jnp.dot(h, w_ref[:, (3 + j) * w:(4 + j) * w], preferred_element_type=F32)
        sg_ref[:, j * w:(j + 1) * w] = _sigmoid(gj).astype(BF16)


def _inproj_conv(xc, modsel, norm_g, w, ntile):
    t = xc.shape[0]
    tm = TOKEN_TILE
    tok = lambda n: pl.BlockSpec((tm, n), lambda i: (i, 0))
    return pl.pallas_call(
        _inproj_conv_kernel,
        grid=(t // tm,),
        in_specs=[tok(D_MODEL), _mod_spec(ntile), _const_spec((1, D_MODEL)), _const_spec((D_MODEL, 6 * C_WIDTH))],
        out_specs=[tok(C_WIDTH), tok(C_WIDTH), tok(3 * D_MODEL)],
        out_shape=[jax.ShapeDtypeStruct((t, C_WIDTH), BF16), jax.ShapeDtypeStruct((t, C_WIDTH), BF16),
                   jax.ShapeDtypeStruct((t, 3 * D_MODEL), BF16)],
        compiler_params=_params(("parallel",)),
        name="inproj_conv",
    )(xc, modsel, norm_g, w)


def _mlstm_kernel(k_ref, qt_ref, vt_ref, g_ref, gt_ref, h_ref, ct_ref, nt_ref, m_ref):
    d = pl.program_id(0)
    j = pl.program_id(2)
    L = M_CHUNK

    @pl.when(j == 0)
    def _():
        ct_ref[...] = jnp.zeros_like(ct_ref)
        nt_ref[...] = jnp.zeros_like(nt_ref)
        m_ref[...] = jnp.zeros_like(m_ref)

    sign = 1 - 2 * d
    row = lax.broadcasted_iota(jnp.int32, (L, L), 0)
    col = lax.broadcasted_iota(jnp.int32, (L, L), 1)
    mask = (col - row) * sign >= 0
    mask_b = jnp.where(mask, 1.0, 0.0).astype(BF16)
    maskt_b = jnp.where((row - col) * sign >= 0, 1.0, 0.0).astype(BF16)
    g = g_ref[...]
    gt = gt_ref[...]
    cum_col = sum(jnp.dot(maskt_b, p, preferred_element_type=F32) for p in _split3(g))
    cum_row = sum(jnp.dot(p, mask_b, preferred_element_type=F32) for p in _split3(gt))
    tot = jnp.sum(gt, axis=-1, keepdims=True)
    src = cum_col - pltpu.roll(g, M_HEADS, 1)

    for hd in range(M_HEADS):
        fi = M_HEADS + hd
        bt = cum_row[fi:fi + 1, :]
        it = gt[hd:hd + 1, :]
        total = tot[fi:fi + 1, :]
        m_old = m_ref[hd]
        dmat = jnp.where(mask, bt - src[:, fi:fi + 1], -jnp.inf)
        inter = bt + m_old
        m_t = jnp.maximum(inter, jnp.max(dmat, axis=0, keepdims=True))
        w_inter = jnp.exp(inter - m_t)
        kh = k_ref[:, hd * M_DQK:(hd + 1) * M_DQK]
        qth = qt_ref[hd * M_DQK:(hd + 1) * M_DQK, :]
        vth = vt_ref[hd * M_DV:(hd + 1) * M_DV, :]
        s = jnp.dot(kh, qth, preferred_element_type=F32) * jnp.exp(dmat - m_t)
        ct_old = ct_ref[hd]
        nt_old = nt_ref[hd]
        num = (jnp.dot(vth, s.astype(BF16), preferred_element_type=F32)
               + w_inter * jnp.dot(ct_old.astype(BF16), qth, preferred_element_type=F32))
        n_hi = nt_old.astype(BF16)
        n_lo = (nt_old - n_hi.astype(F32)).astype(BF16)
        qn = (jnp.dot(n_hi, qth, preferred_element_type=F32)
              + jnp.dot(n_lo, qth, preferred_element_type=F32))[0:1, :]
        den = jnp.sum(s, axis=0, keepdims=True) + w_inter * qn
        inv = 1.0 / jnp.maximum(jnp.abs(den), jnp.exp(-m_t))
        h_ref[:, hd * M_DV:(hd + 1) * M_DV] = (num * inv).T.astype(BF16)

        grow = total - bt + it
        m_new = jnp.maximum(total + m_old[:, 0:1], jnp.max(grow, axis=-1, keepdims=True))
        w_g = jnp.exp(grow - m_new)
        w_c = jnp.exp(total + m_old[:, 0:1] - m_new)
        vw = vth * w_g.astype(BF16)
        ct_ref[hd] = w_c * ct_old + jnp.dot(vw, kh, preferred_element_type=F32)
        wg8 = jnp.broadcast_to(w_g, (SUBLANES, L)).astype(BF16)
        nt_ref[hd] = w_c * nt_old + jnp.dot(wg8, kh, preferred_element_type=F32)
        m_ref[hd] = jnp.broadcast_to(m_new, (1, L))


def _mlstm(k, qt, vt, gates, gates_t, bsz, nch):
    t = k.shape[0]
    L = M_CHUNK

    def chunk(d, b, j):
        return b * nch + jnp.where(d == 0, j, jnp.where(j == 0, 0, nch - j))

    return pl.pallas_call(
        _mlstm_kernel,
        grid=(2, bsz, nch),
        in_specs=[
            pl.BlockSpec((L, M_QK), lambda d, b, j: (chunk(d, b, j), 0)),
            pl.BlockSpec((M_QK, L), lambda d, b, j: (0, chunk(d, b, j))),
            pl.BlockSpec((M_V, L), lambda d, b, j: (0, chunk(d, b, j))),
            pl.BlockSpec((None, L, LANES), lambda d, b, j: (d, chunk(d, b, j), 0)),
            pl.BlockSpec((None, 2 * M_HEADS, L), lambda d, b, j: (d, 0, chunk(d, b, j))),
        ],
        out_specs=pl.BlockSpec((None, L, M_V), lambda d, b, j: (d, chunk(d, b, j), 0)),
        out_shape=jax.ShapeDtypeStruct((2, t, M_V), BF16),
        scratch_shapes=[pltpu.VMEM((M_HEADS, M_DV, M_DQK), F32), pltpu.VMEM((M_HEADS, SUBLANES, M_DQK), F32),
                        pltpu.VMEM((M_HEADS, 1, L), F32)],
        compiler_params=_params(("arbitrary", "arbitrary", "arbitrary")),
        name="mlstm_scan",
    )(k, qt, vt, gates, gates_t)


def _attn_kernel(q_ref, k_ref, vt_ref, o_ref, acc_ref, s_ref, *, nkt, sub):
    tq = q_ref.shape[0]
    tt = vt_ref.shape[2]
    tk = sub * tt
    qs = [q_ref[:, g * A_HD:(g + 1) * A_HD] for g in range(A_GROUP)]

    def scores(c, g, j, slot):
        kj = k_ref[pl.ds(pl.multiple_of(c * tk + j * tt, tt), tt), :]
        s = lax.dot_general(kj, qs[g], NT_DIMS, preferred_element_type=F32)
        s_ref[slot, j * tt:(j + 1) * tt, :] = s
        return jnp.max(s.reshape(tt // SUBLANES, SUBLANES, tq), axis=0)

    def colmax(cm8):
        return jnp.max(cm8, axis=0, keepdims=True)

    cm8 = scores(0, 0, 0, 0)
    for j in range(1, sub):
        cm8 = jnp.maximum(cm8, scores(0, 0, j, 0))

    def trip(c, carry, last):
        ms, ls, cm = list(carry[0]), list(carry[1]), carry[2]
        for g in range(A_GROUP):
            nxt = (c, g + 1) if g + 1 < A_GROUP else (None if last else (c + 1, 0))
            m_new = jnp.maximum(ms[g], cm)
            alpha = jnp.exp2(ms[g] - m_new)
            l8 = None
            pv = None
            cm8n = None
            for j in range(sub):
                if nxt is not None:
                    c8 = scores(nxt[0], nxt[1], j, (g + 1) % 2)
                    cm8n = c8 if cm8n is None else jnp.maximum(cm8n, c8)
                p = jnp.exp2(s_ref[g % 2, j * tt:(j + 1) * tt, :] - m_new)
                p8 = jnp.sum(p.reshape(tt // SUBLANES, SUBLANES, tq), axis=0)
                l8 = p8 if l8 is None else l8 + p8
                d = jnp.dot(vt_ref[c * sub + j], p.astype(BF16), preferred_element_type=F32)
                pv = d if pv is None else pv + d
            acc_ref[g] = alpha * acc_ref[g] + pv
            ls[g] = alpha * ls[g] + l8
            ms[g] = m_new
            if nxt is not None:
                cm = colmax(cm8n)
        return tuple(ms), tuple(ls), cm

    acc_ref[...] = jnp.zeros_like(acc_ref)
    carry = (tuple(jnp.full((1, tq), -jnp.inf, F32) for _ in range(A_GROUP)),
             tuple(jnp.zeros((SUBLANES, tq), F32) for _ in range(A_GROUP)), colmax(cm8))
    carry = lax.fori_loop(0, nkt - 1, lambda c, cr: trip(c, cr, False), carry)
    _, ls, _ = trip(nkt - 1, carry, True)
    for g in range(A_GROUP):
        l = jnp.sum(ls[g], axis=0, keepdims=True)
        o_ref[:, g * A_HD:(g + 1) * A_HD] = (acc_ref[g] / l).T.astype(BF16)


def _attention(aq, ak, avt, bsz, ntile, q_off, nq, nk):
    tq = TOKEN_TILE
    tt = avt.shape[2]
    sub = min(ATTN_KV_TILE, nk) // tt
    tk = sub * tt
    nt = ntile * tq
    assert nt % nk == 0 and nk % tk == 0
    return pl.pallas_call(
        functools.partial(_attn_kernel, nkt=nk // tk, sub=sub),
        grid=(bsz, A_KV_HEADS, nq),
        in_specs=[
            pl.BlockSpec((tq, A_GROUP * A_HD), lambda b, h, i: (b * ntile + q_off + i, h)),
            pl.BlockSpec((nk, A_HD), lambda b, h, i: (b * (nt // nk), h)),
            pl.BlockSpec((nk // tt, A_HD, tt), lambda b, h, i: (b * (nt // nk), h, 0)),
        ],
        out_specs=pl.BlockSpec((tq, A_GROUP * A_HD), lambda b, h, i: (b * nq + i, h)),
        out_shape=jax.ShapeDtypeStruct((bsz * nq * tq, A_Q), BF16),
        scratch_shapes=[pltpu.VMEM((A_GROUP, A_HD, tq), F32), pltpu.VMEM((2, tk, tq), F32)],
        compiler_params=_params(("parallel", "parallel", "arbitrary")),
        name="gqa_attention",
    )(aq, ak, avt)


def _mix_kernel(x_ref, mod_ref, hf_ref, hr_ref, o_ref, mg_ref, yac_ref, yal_ref, cu_ref, cup_ref, cun_ref, b_ref,
                cw_ref, sg_ref, wpm_ref, wpa_ref, wpc_ref, wo_ref, n2_ref, wup_ref, wdn_ref, fg_ref, out_ref,
                *, ntile, tile_off, final):
    pos = pl.program_id(1) + tile_off
    tm = x_ref.shape[0]
    hsum = hf_ref[...].astype(F32) + hr_ref[...].astype(F32)
    parts = []
    for hd in range(M_HEADS):
        hh = hsum[:, hd * M_DV:(hd + 1) * M_DV]
        parts.append(hh * lax.rsqrt(jnp.mean(hh * hh, axis=-1, keepdims=True) + EPS))
    hn = jnp.concatenate(parts, axis=-1) * mg_ref[...]
    ym = (hn * _sigmoid(o_ref[...].astype(F32))).astype(BF16)
    ya = jnp.where(pos == 0, yac_ref[...], yal_ref[...])
    has_prev = pos >= 2
    has_next = jnp.logical_and(pos >= 1, pos <= ntile - 2)
    cu = cu_ref[...].astype(F32)
    prev_row = jnp.where(has_prev, cup_ref[BF16_SUBLANES - 1:BF16_SUBLANES, :].astype(F32), 0.0)
    next_row = jnp.where(has_next, cun_ref[0:1, :].astype(F32), 0.0)
    ridx = lax.broadcasted_iota(jnp.int32, cu.shape, 0)
    up = jnp.where(ridx == 0, prev_row, pltpu.roll(cu, 1, 0))
    dn = jnp.where(ridx == tm - 1, next_row, pltpu.roll(cu, tm - 1, 0))
    conv = cw_ref[0:1, :] * up + cw_ref[1:2, :] * cu + cw_ref[2:3, :] * dn
    yc = (b_ref[...].astype(F32) * conv).astype(BF16)
    w = D_MODEL
    merged = (sg_ref[:, 0:w].astype(F32) * jnp.dot(ym, wpm_ref[...], preferred_element_type=F32)
              + sg_ref[:, w:2 * w].astype(F32) * jnp.dot(ya, wpa_ref[...], preferred_element_type=F32)
              + sg_ref[:, 2 * w:3 * w].astype(F32) * jnp.dot(yc, wpc_ref[...], preferred_element_type=F32))
    out = jnp.dot(merged.astype(BF16), wo_ref[...], preferred_element_type=F32)
    x1 = x_ref[...] + mod_ref[2:3, :] * out
    h = _norm_mod(x1, n2_ref[...], mod_ref[3:4, :], mod_ref[4:5, :]).astype(BF16)
    acc = jnp.zeros(x1.shape, F32)
    fc = D_FF // 4
    for c in range(4):
        a = jnp.dot(h, wup_ref[:, c * fc:(c + 1) * fc], preferred_element_type=F32)
        a = jnp.square(jnp.maximum(a, 0.0)).astype(BF16)
        acc = acc + jnp.dot(a, wdn_ref[c * fc:(c + 1) * fc, :], preferred_element_type=F32)
    y = x1 + mod_ref[5:6, :] * acc
    if final:
        ms = jnp.mean(y * y, axis=-1, keepdims=True)
        y = y * lax.rsqrt(ms + EPS) * fg_ref[...]
    out_ref[...] = y


def _mix(xc, modsel, h2, o, mgain, ya_ctx, ya_lat, cu, bg, conv_w, sg, wpm, wpa, wpc, wo, n2, wup, wdn, final_g,
         bsz, ntile, tile_off, final):
    t = xc.shape[0]
    tm = TOKEN_TILE
    hb = tm // BF16_SUBLANES
    nhb = t // BF16_SUBLANES
    nout = ntile - tile_off
    tile = lambda b, i: b * ntile + tile_off + i
    tok = lambda n: pl.BlockSpec((tm, n), lambda b, i: (tile(b, i), 0))
    return pl.pallas_call(
        functools.partial(_mix_kernel, ntile=ntile, tile_off=tile_off, final=final),
        grid=(bsz, nout),
        in_specs=[
            tok(D_MODEL),
            pl.BlockSpec((None, None, N_MOD, D_MODEL), lambda b, i: (b, jnp.minimum(i + tile_off, 1), 0, 0)),
            pl.BlockSpec((None, tm, M_V), lambda b, i: (0, tile(b, i), 0)),
            pl.BlockSpec((None, tm, M_V), lambda b, i: (1, tile(b, i), 0)),
            tok(M_V),
            _const_spec((1, M_V)),
            pl.BlockSpec((tm, A_Q), lambda b, i: (b, 0)),
            pl.BlockSpec((tm, A_Q), lambda b, i: (b * (ntile - 1) + jnp.maximum(i + tile_off - 1, 0), 0)),
            tok(C_WIDTH),
            pl.BlockSpec((BF16_SUBLANES, C_WIDTH), lambda b, i: (jnp.maximum(tile(b, i) * hb - 1, 0), 0)),
            pl.BlockSpec((BF16_SUBLANES, C_WIDTH), lambda b, i: (jnp.minimum((tile(b, i) + 1) * hb, nhb - 1), 0)),
            tok(C_WIDTH),
            _const_spec((3, C_WIDTH)),
            tok(3 * D_MODEL),
            _const_spec((M_V, D_MODEL)), _const_spec((A_Q, D_MODEL)), _const_spec((C_WIDTH, D_MODEL)),
            _const_spec((D_MODEL, D_MODEL)),
            _const_spec((1, D_MODEL)),
            _const_spec((D_MODEL, D_FF)),
            _const_spec((D_FF, D_MODEL)),
            _const_spec((1, D_MODEL)),
        ],
        out_specs=pl.BlockSpec((tm, D_MODEL), lambda b, i: (b * nout + i, 0)),
        out_shape=jax.ShapeDtypeStruct((bsz * nout * tm, D_MODEL), F32),
        compiler_params=_params(("parallel", "parallel")),
        name="mix_and_mlp",
    )(xc, modsel, h2, h2, o, mgain, ya_ctx, ya_lat, cu, cu, cu, bg, conv_w, sg, wpm, wpa, wpc, wo, n2, wup, wdn,
      final_g)


def _rope_perm():
    p = np.arange(A_HD)
    part, axis, f = p // 64, (p // 32) % 2, p % 32
    return axis * 64 + part * 32 + f


def _rope_tables(n_ctx, n_lat):
    rows = n_lat // GRID_W
    row_idx = jnp.repeat(jnp.arange(rows), GRID_W)
    col_idx = jnp.tile(jnp.arange(GRID_W), rows)
    n_freq = A_HD // 4
    inv_freq = ROPE_THETA ** (-jnp.arange(n_freq, dtype=F32) / n_freq)
    pos = jnp.stack([row_idx, col_idx], axis=-1).astype(F32)
    ang = pos[:, :, None] * inv_freq
    cos = jnp.cos(ang).reshape(n_lat, 2 * n_freq)
    sin = jnp.sin(ang).reshape(n_lat, 2 * n_freq)
    cos_t = jnp.concatenate([cos, cos], axis=-1)
    sin_t = jnp.concatenate([-sin, sin], axis=-1)
    cos_t = jnp.concatenate([jnp.ones((n_ctx, A_HD), F32), cos_t], axis=0)
    sin_t = jnp.concatenate([jnp.zeros((n_ctx, A_HD), F32), sin_t], axis=0)
    return cos_t, sin_t


def kernel(x, c, ctx, c_ctx, w_ada, b_ada, norm1, norm2, w_in, mlstm_gate_b, mlstm_norm, q_norm, k_norm,
           conv_w, w_proj_m, w_proj_a, w_proj_c, w_out, w_up, w_down, final_norm):
    bsz, n_lat, _ = x.shape
    n_ctx = ctx.shape[1]
    depth = w_in.shape[0]
    assert n_ctx == TOKEN_TILE and n_ctx == M_CHUNK and n_lat % TOKEN_TILE == 0 and bsz + 1 <= 8
    nt = n_ctx + n_lat
    ntile = nt // TOKEN_TILE
    nch = nt // M_CHUNK

    xc = jnp.concatenate([ctx, x], axis=1).reshape(bsz * nt, D_MODEL)
    c_rows = jnp.zeros((8, D_MODEL), F32).at[:bsz].set(c).at[bsz].set(c_ctx)
    mods = _modulation(c_rows, w_ada, b_ada).reshape(depth, 8, N_MOD, D_MODEL)
    cos_t, sin_t = _rope_tables(n_ctx, n_lat)
    perm = _rope_perm()
    head_perm = lambda n: np.concatenate([h * A_HD + perm for h in range(n)])
    fg = final_norm.reshape(1, D_MODEL)

    for layer in range(depth):
        need_ctx = layer < depth - 1
        tile_off = 0 if need_ctx else 1
        ml = mods[layer]
        modsel = jnp.stack([jnp.broadcast_to(ml[bsz], (bsz, N_MOD, D_MODEL)), ml[:bsz]], axis=1)
        wl = w_in[layer]
        o = COL_OFF
        w_m = jnp.concatenate([wl[:, o[1]:o[2]], wl[:, o[3]:o[4]]], axis=1).astype(BF16)
        w_mt = jnp.concatenate([wl[:, o[0]:o[1]], wl[:, o[2]:o[3]]], axis=1).T.astype(BF16)
        w_g = wl[:, o[4]:o[5]]
        w_gp = jnp.zeros((D_MODEL, 2 * LANES), F32)
        w_gp = w_gp.at[:, 0:2 * M_HEADS].set(w_g[:, 0:2 * M_HEADS])
        w_gp = w_gp.at[:, LANES:LANES + 2 * M_HEADS].set(w_g[:, 2 * M_HEADS:]).astype(BF16)
        w_gt = w_g.T.astype(BF16)
        gb = mlstm_gate_b[layer].reshape(M_GATES)
        gbp = jnp.zeros((1, 2 * LANES), F32)
        gbp = gbp.at[0, 0:2 * M_HEADS].set(gb[0:2 * M_HEADS]).at[0, LANES:LANES + 2 * M_HEADS].set(gb[2 * M_HEADS:])
        gbt = gb.reshape(M_GATES, 1)
        w_a = jnp.concatenate([wl[:, o[5]:o[6]][:, head_perm(A_HEADS)], wl[:, o[6]:o[7]][:, head_perm(A_KV_HEADS)]],
                              axis=1).astype(BF16)
        w_vt = wl[:, o[7]:o[8]].T.astype(BF16)
        w_c = wl[:, o[8]:o[14]].astype(BF16)
        n1 = norm1[layer].reshape(1, D_MODEL)
        n2 = norm2[layer].reshape(1, D_MODEL)

        mk, mqt, mvt, mo, gates, gates_t = _inproj_mlstm(xc, modsel, n1, w_m, w_mt, w_gp, w_gt, gbp, gbt, ntile)
        aq, ak, avt = _inproj_attn(xc, modsel, n1, w_a, w_vt, q_norm[layer][perm].reshape(1, A_HD),
                                   k_norm[layer][perm].reshape(1, A_HD), cos_t, sin_t, ntile)
        cu, bg, sg = _inproj_conv(xc, modsel, n1, w_c, ntile)

        h2 = _mlstm(mk, mqt, mvt, gates, gates_t, bsz, nch)
        ya_lat = _attention(aq, ak, avt, bsz, ntile, 1, ntile - 1, nt)
        ya_ctx = _attention(aq, ak, avt, bsz, ntile, 0, 1, n_ctx) if need_ctx else ya_lat

        xc = _mix(xc, modsel, h2, mo, mlstm_norm[layer].reshape(1, M_V), ya_ctx, ya_lat, cu, bg, conv_w[layer], sg,
                  w_proj_m[layer].astype(BF16), w_proj_a[layer].astype(BF16), w_proj_c[layer].astype(BF16),
                  w_out[layer].astype(BF16), n2, w_up[layer].astype(BF16), w_down[layer].astype(BF16), fg,
                  bsz, ntile, tile_off, not need_ctx)

    return xc.reshape(bsz, n_lat, D_MODEL)
```

```python
import functools

import numpy as np
import jax
import jax.numpy as jnp
from jax import lax
from jax.experimental import pallas as pl
from jax.experimental.pallas import tpu as pltpu

D_MODEL = 1024
GRID_W = 64
M_HEADS = 4
M_DQK = 128
M_DV = 256
M_QK = M_HEADS * M_DQK
M_V = M_HEADS * M_DV
M_GATES = 2 * 2 * M_HEADS
A_HEADS = 8
A_KV_HEADS = 2
A_GROUP = A_HEADS // A_KV_HEADS
A_HD = 128
A_Q = A_HEADS * A_HD
A_KV = A_KV_HEADS * A_HD
ROPE_THETA = 10000.0
C_WIDTH = D_MODEL
D_FF = 4 * D_MODEL
N_MOD = 6
EPS = 1e-6
COL_SIZES = (M_QK, M_QK, M_V, M_V, M_GATES, A_Q, A_KV, A_KV, C_WIDTH, C_WIDTH, C_WIDTH, D_MODEL, D_MODEL, D_MODEL)
COL_OFF = tuple(int(v) for v in np.concatenate([[0], np.cumsum(COL_SIZES)]))

LANES = 128
SUBLANES = 8
BF16_SUBLANES = 16
TOKEN_TILE = 256
M_CHUNK = 256
ATTN_KV_TILE = 2816
ATTN_Q_TILES = 2
VMEM_LIMIT = 56 * 1024 * 1024

F32 = jnp.float32
LOG2_E = 1.4426950408889634
BF16 = jnp.bfloat16
NT_DIMS = (((1,), (1,)), ((), ()))


def _params(sem):
    return pltpu.CompilerParams(dimension_semantics=sem, vmem_limit_bytes=VMEM_LIMIT)


def _const_spec(shape):
    nd = len(shape)
    return pl.BlockSpec(shape, lambda *_: (0,) * nd, pipeline_mode=pl.Buffered(1))


def _norm_mod(x, g, shift, scale):
    ms = jnp.mean(x * x, axis=-1, keepdims=True)
    y = x * lax.rsqrt(ms + EPS) * g
    return y * (1.0 + scale) + shift


def _sigmoid(x):
    return 1.0 / (1.0 + jnp.exp(-x))


def _log_sigmoid(x):
    return jnp.minimum(x, 0.0) - jnp.log(1.0 + jnp.exp(-jnp.abs(x)))


def _split3(x):
    hi = x.astype(BF16)
    r = x - hi.astype(F32)
    mid = r.astype(BF16)
    lo = (r - mid.astype(F32)).astype(BF16)
    return hi, mid, lo


def _mod_kernel(c_ref, w_ref, b_ref, o_ref):
    c = c_ref[...]
    s = c * _sigmoid(c)
    o_ref[...] = jnp.dot(s, w_ref[...], precision=lax.Precision.HIGHEST,
                         preferred_element_type=F32) + b_ref[...]


def _modulation(c_rows, w_ada, b_ada):
    depth = w_ada.shape[0]
    tn = 1536
    return pl.pallas_call(
        _mod_kernel,
        grid=(depth, N_MOD * D_MODEL // tn),
        in_specs=[
            pl.BlockSpec((8, D_MODEL), lambda l, j: (0, 0)),
            pl.BlockSpec((None, D_MODEL, tn), lambda l, j: (l, 0, j)),
            pl.BlockSpec((None, 1, tn), lambda l, j: (l, 0, j)),
        ],
        out_specs=pl.BlockSpec((None, 8, tn), lambda l, j: (l, 0, j)),
        out_shape=jax.ShapeDtypeStruct((depth, 8, N_MOD * D_MODEL), F32),
        compiler_params=_params(("arbitrary", "arbitrary")),
        name="modulation",
    )(c_rows, w_ada, b_ada.reshape(depth, 1, N_MOD * D_MODEL))


def _mod_spec(ntile):
    return pl.BlockSpec((None, None, N_MOD, D_MODEL), lambda i: (i // ntile, jnp.minimum(i % ntile, 1), 0, 0))


def _inproj_mlstm_kernel(x_ref, mod_ref, g_ref, w_ref, wt_ref, wg_ref, wgt_ref, gb_ref, gbt_ref,
                         k_ref, qt_ref, vt_ref, o_ref, gate_ref, gatet_ref):
    h = _norm_mod(x_ref[...], g_ref[...], mod_ref[0:1, :], mod_ref[1:2, :]).astype(BF16)
    k_ref[...] = jnp.dot(h, w_ref[:, 0:M_QK], preferred_element_type=F32).astype(BF16)
    o_ref[...] = jnp.dot(h, w_ref[:, M_QK:], preferred_element_type=F32).astype(BF16)
    qt = lax.dot_general(wt_ref[0:M_QK, :], h, NT_DIMS, preferred_element_type=F32)
    qt_ref[...] = (qt * (M_DQK ** -0.5)).astype(BF16)
    vt_ref[...] = lax.dot_general(wt_ref[M_QK:, :], h, NT_DIMS, preferred_element_type=F32).astype(BF16)
    ga = jnp.dot(h, wg_ref[...], preferred_element_type=F32) + gb_ref[...]
    lane = lax.broadcasted_iota(jnp.int32, ga.shape, 1) % LANES
    ga = jnp.where(lane >= M_HEADS, _log_sigmoid(ga), ga)
    gate_ref[0] = ga[:, :LANES]
    gate_ref[1] = ga[:, LANES:]
    gt = lax.dot_general(wgt_ref[...], h, NT_DIMS, preferred_element_type=F32) + gbt_ref[...]
    row = lax.broadcasted_iota(jnp.int32, gt.shape, 0) % (2 * M_HEADS)
    gt = jnp.where(row >= M_HEADS, _log_sigmoid(gt), gt)
    gatet_ref[0] = gt[:2 * M_HEADS]
    gatet_ref[1] = gt[2 * M_HEADS:]


def _inproj_mlstm(xc, modsel, norm_g, w, wt, wg, wgt, gb, gbt, ntile):
    t = xc.shape[0]
    tm = TOKEN_TILE
    tok = lambda n: pl.BlockSpec((tm, n), lambda i: (i, 0))
    feat = lambda n: pl.BlockSpec((n, tm), lambda i: (0, i))
    return pl.pallas_call(
        _inproj_mlstm_kernel,
        grid=(t // tm,),
        in_specs=[
            tok(D_MODEL), _mod_spec(ntile), _const_spec((1, D_MODEL)),
            _const_spec((D_MODEL, M_QK + M_V)),
            _const_spec((M_QK + M_V, D_MODEL)),
            _const_spec((D_MODEL, 2 * LANES)),
            _const_spec((M_GATES, D_MODEL)),
            _const_spec((1, 2 * LANES)),
            _const_spec((M_GATES, 1)),
        ],
        out_specs=[
            tok(M_QK), feat(M_QK), feat(M_V), tok(M_V),
            pl.BlockSpec((2, tm, LANES), lambda i: (0, i, 0)),
            pl.BlockSpec((2, 2 * M_HEADS, tm), lambda i: (0, 0, i)),
        ],
        out_shape=[
            jax.ShapeDtypeStruct((t, M_QK), BF16), jax.ShapeDtypeStruct((M_QK, t), BF16),
            jax.ShapeDtypeStruct((M_V, t), BF16), jax.ShapeDtypeStruct((t, M_V), BF16),
            jax.ShapeDtypeStruct((2, t, LANES), F32),
            jax.ShapeDtypeStruct((2, 2 * M_HEADS, t), F32),
        ],
        compiler_params=_params(("parallel",)),
        name="inproj_mlstm",
    )(xc, modsel, norm_g, w, wt, wg, wgt, gb, gbt)


def _head_norm_rope(xh, gain, cos, sin):
    ms = jnp.mean(xh * xh, axis=-1, keepdims=True)
    y = xh * lax.rsqrt(ms + EPS) * gain
    return y * cos + pltpu.roll(y, A_HD // 2, 1) * sin


def _inproj_attn_kernel(x_ref, mod_ref, g_ref, w_ref, wvt_ref, qg_ref, kg_ref, cos_ref, sin_ref,
                        q_ref, k_ref, vt_ref):
    h = _norm_mod(x_ref[...], g_ref[...], mod_ref[0:1, :], mod_ref[1:2, :]).astype(BF16)
    cos = cos_ref[...]
    sin = sin_ref[...]
    for pr in range((A_HEADS + A_KV_HEADS) // 2):
        two = jnp.dot(h, w_ref[:, pr * 2 * A_HD:(pr + 1) * 2 * A_HD], preferred_element_type=F32)
        for u in range(2):
            hd = 2 * pr + u
            xh = two[:, u * A_HD:(u + 1) * A_HD]
            if hd < A_HEADS:
                qh = _head_norm_rope(xh, qg_ref[...], cos, sin) * (A_HD ** -0.5 * LOG2_E)
                q_ref[:, hd * A_HD:(hd + 1) * A_HD] = qh.astype(BF16)
            else:
                kd = hd - A_HEADS
                k_ref[:, kd * A_HD:(kd + 1) * A_HD] = _head_norm_rope(xh, kg_ref[...], cos, sin).astype(BF16)
    vt_ref[...] = lax.dot_general(wvt_ref[...], h, NT_DIMS, preferred_element_type=F32).astype(BF16)


def _inproj_attn(xc, modsel, norm_g, w, wvt, qg, kg, cos, sin, ntile):
    t = xc.shape[0]
    tm = TOKEN_TILE
    tok = lambda n: pl.BlockSpec((tm, n), lambda i: (i, 0))
    return pl.pallas_call(
        _inproj_attn_kernel,
        grid=(t // tm,),
        in_specs=[
            tok(D_MODEL), _mod_spec(ntile), _const_spec((1, D_MODEL)),
            _const_spec((D_MODEL, A_Q + A_KV)),
            _const_spec((A_KV, D_MODEL)),
            _const_spec((1, A_HD)), _const_spec((1, A_HD)),
            pl.BlockSpec((tm, A_HD), lambda i: (i % ntile, 0)),
            pl.BlockSpec((tm, A_HD), lambda i: (i % ntile, 0)),
        ],
        out_specs=[tok(A_Q), tok(A_KV), pl.BlockSpec((None, A_KV, tm), lambda i: (i, 0, 0))],
        out_shape=[jax.ShapeDtypeStruct((t, A_Q), BF16), jax.ShapeDtypeStruct((t, A_KV), BF16),
                   jax.ShapeDtypeStruct((t // tm, A_KV, tm), BF16)],
        compiler_params=_params(("parallel",)),
        name="inproj_attn",
    )(xc, modsel, norm_g, w, wvt, qg, kg, cos, sin)


def _inproj_conv_kernel(x_ref, mod_ref, g_ref, w_ref, cu_ref, b_ref, sg_ref):
    h = _norm_mod(x_ref[...], g_ref[...], mod_ref[0:1, :], mod_ref[1:2, :]).astype(BF16)
    w = C_WIDTH
    u = jnp.dot(h, w_ref[:, 0:w], preferred_element_type=F32)
    cg = jnp.dot(h, w_ref[:, 2 * w:3 * w], preferred_element_type=F32)
    cu_ref[...] = (cg * u).astype(BF16)
    b_ref[...] = jnp.dot(h, w_ref[:, w:2 * w], preferred_element_type=F32).astype(BF16)
    for j in range(3):
        gj = jnp.dot(h, w_ref[:, (3 + j) * w:(4 + j) * w], preferred_element_type=F32)
        sg_ref[:, j * w:(j + 1) * w] = _sigmoid(gj).astype(BF16)


def _inproj_conv(xc, modsel, norm_g, w, ntile):
    t = xc.shape[0]
    tm = TOKEN_TILE
    tok = lambda n: pl.BlockSpec((tm, n), lambda i: (i, 0))
    return pl.pallas_call(
        _inproj_conv_kernel,
        grid=(t // tm,),
        in_specs=[tok(D_MODEL), _mod_spec(ntile), _const_spec((1, D_MODEL)), _const_spec((D_MODEL, 6 * C_WIDTH))],
        out_specs=[tok(C_WIDTH), tok(C_WIDTH), tok(3 * D_MODEL)],
        out_shape=[jax.ShapeDtypeStruct((t, C_WIDTH), BF16), jax.ShapeDtypeStruct((t, C_WIDTH), BF16),
                   jax.ShapeDtypeStruct((t, 3 * D_MODEL), BF16)],
        compiler_params=_params(("parallel",)),
        name="inproj_conv",
    )(xc, modsel, norm_g, w)


def _mlstm_kernel(k_ref, qt_ref, vt_ref, g_ref, gt_ref, h_ref, ct_ref, nt_ref, m_ref):
    d = pl.program_id(0)
    j = pl.program_id(2)
    L = M_CHUNK

    @pl.when(j == 0)
    def _():
        ct_ref[...] = jnp.zeros_like(ct_ref)
        nt_ref[...] = jnp.zeros_like(nt_ref)
        m_ref[...] = jnp.zeros_like(m_ref)

    sign = 1 - 2 * d
    row = lax.broadcasted_iota(jnp.int32, (L, L), 0)
    col = lax.broadcasted_iota(jnp.int32, (L, L), 1)
    mask = (col - row) * sign >= 0
    mask_b = jnp.where(mask, 1.0, 0.0).astype(BF16)
    maskt_b = jnp.where((row - col) * sign >= 0, 1.0, 0.0).astype(BF16)
    g = g_ref[...]
    gt = gt_ref[...]
    cum_col = sum(jnp.dot(maskt_b, p, preferred_element_type=F32) for p in _split3(g))
    cum_row = sum(jnp.dot(p, mask_b, preferred_element_type=F32) for p in _split3(gt))
    tot = jnp.sum(gt, axis=-1, keepdims=True)
    src = cum_col - pltpu.roll(g, M_HEADS, 1)

    for hd in range(M_HEADS):
        fi = M_HEADS + hd
        bt = cum_row[fi:fi + 1, :]
        it = gt[hd:hd + 1, :]
        total = tot[fi:fi + 1, :]
        m_old = m_ref[hd]
        dmat = jnp.where(mask, bt - src[:, fi:fi + 1], -jnp.inf)
        inter = bt + m_old
        m_t = jnp.maximum(inter, jnp.max(dmat, axis=0, keepdims=True))
        w_inter = jnp.exp(inter - m_t)
        kh = k_ref[:, hd * M_DQK:(hd + 1) * M_DQK]
        qth = qt_ref[hd * M_DQK:(hd + 1) * M_DQK, :]
        vth = vt_ref[hd * M_DV:(hd + 1) * M_DV, :]
        s = jnp.dot(kh, qth, preferred_element_type=F32) * jnp.exp(dmat - m_t)
        ct_old = ct_ref[hd]
        nt_old = nt_ref[hd]
        num = (jnp.dot(vth, s.astype(BF16), preferred_element_type=F32)
               + w_inter * jnp.dot(ct_old.astype(BF16), qth, preferred_element_type=F32))
        n_hi = nt_old.astype(BF16)
        n_lo = (nt_old - n_hi.astype(F32)).astype(BF16)
        qn = (jnp.dot(n_hi, qth, preferred_element_type=F32)
              + jnp.dot(n_lo, qth, preferred_element_type=F32))[0:1, :]
        den = jnp.sum(s, axis=0, keepdims=True) + w_inter * qn
        inv = 1.0 / jnp.maximum(jnp.abs(den), jnp.exp(-m_t))
        h_ref[:, hd * M_DV:(hd + 1) * M_DV] = (num * inv).T.astype(BF16)

        grow = total - bt + it
        m_new = jnp.maximum(total + m_old[:, 0:1], jnp.max(grow, axis=-1, keepdims=True))
        w_g = jnp.exp(grow - m_new)
        w_c = jnp.exp(total + m_old[:, 0:1] - m_new)
        vw = vth * w_g.astype(BF16)
        ct_ref[hd] = w_c * ct_old + jnp.dot(vw, kh, preferred_element_type=F32)
        wg8 = jnp.broadcast_to(w_g, (SUBLANES, L)).astype(BF16)
        nt_ref[hd] = w_c * nt_old + jnp.dot(wg8, kh, preferred_element_type=F32)
        m_ref[hd] = jnp.broadcast_to(m_new, (1, L))


def _mlstm(k, qt, vt, gates, gates_t, bsz, nch):
    t = k.shape[0]
    L = M_CHUNK

    def chunk(d, b, j):
        return b * nch + jnp.where(d == 0, j, jnp.where(j == 0, 0, nch - j))

    return pl.pallas_call(
        _mlstm_kernel,
        grid=(2, bsz, nch),
        in_specs=[
            pl.BlockSpec((L, M_QK), lambda d, b, j: (chunk(d, b, j), 0)),
            pl.BlockSpec((M_QK, L), lambda d, b, j: (0, chunk(d, b, j))),
            pl.BlockSpec((M_V, L), lambda d, b, j: (0, chunk(d, b, j))),
            pl.BlockSpec((None, L, LANES), lambda d, b, j: (d, chunk(d, b, j), 0)),
            pl.BlockSpec((None, 2 * M_HEADS, L), lambda d, b, j: (d, 0, chunk(d, b, j))),
        ],
        out_specs=pl.BlockSpec((None, L, M_V), lambda d, b, j: (d, chunk(d, b, j), 0)),
        out_shape=jax.ShapeDtypeStruct((2, t, M_V), BF16),
        scratch_shapes=[pltpu.VMEM((M_HEADS, M_DV, M_DQK), F32), pltpu.VMEM((M_HEADS, SUBLANES, M_DQK), F32),
                        pltpu.VMEM((M_HEADS, 1, L), F32)],
        compiler_params=_params(("arbitrary", "arbitrary", "arbitrary")),
        name="mlstm_scan",
    )(k, qt, vt, gates, gates_t)


def _attn_kernel(*refs, nkt, sub, nqs):
    q_refs = refs[:nqs]
    k_ref, vt_ref, o_ref, acc_ref, s_ref = refs[nqs:]
    tq = q_refs[0].shape[0]
    tt = vt_ref.shape[2]
    tk = sub * tt
    nit = nqs * A_GROUP
    qs = [q_refs[u][:, g * A_HD:(g + 1) * A_HD] for u in range(nqs) for g in range(A_GROUP)]

    def scores(c, g, j, slot):
        kj = k_ref[pl.ds(pl.multiple_of(c * tk + j * tt, tt), tt), :]
        s = lax.dot_general(kj, qs[g], NT_DIMS, preferred_element_type=F32)
        s_ref[slot, j * tt:(j + 1) * tt, :] = s
        return jnp.max(s.reshape(tt // SUBLANES, SUBLANES, tq), axis=0)

    def colmax(cm8):
        return jnp.max(cm8, axis=0, keepdims=True)

    cm8 = scores(0, 0, 0, 0)
    for j in range(1, sub):
        cm8 = jnp.maximum(cm8, scores(0, 0, j, 0))

    def trip(c, carry, last):
        ms, ls, cm = list(carry[0]), list(carry[1]), carry[2]
        for g in range(nit):
            nxt = (c, g + 1) if g + 1 < nit else (None if last else (c + 1, 0))
            m_new = jnp.maximum(ms[g], cm)
            alpha = jnp.exp2(ms[g] - m_new)
            l8 = None
            pv = None
            cm8n = None
            for j in range(sub):
                if nxt is not None:
                    c8 = scores(nxt[0], nxt[1], j, (g + 1) % 2)
                    cm8n = c8 if cm8n is None else jnp.maximum(cm8n, c8)
                p = jnp.exp2(s_ref[g % 2, j * tt:(j + 1) * tt, :] - m_new)
                p8 = jnp.sum(p.reshape(tt // SUBLANES, SUBLANES, tq), axis=0)
                l8 = p8 if l8 is None else l8 + p8
                d = jnp.dot(vt_ref[c * sub + j], p.astype(BF16), preferred_element_type=F32)
                pv = d if pv is None else pv + d
            acc_ref[g] = alpha * acc_ref[g] + pv
            ls[g] = alpha * ls[g] + l8
            ms[g] = m_new
            if nxt is not None:
                cm = colmax(cm8n)
        return tuple(ms), tuple(ls), cm

    acc_ref[...] = jnp.zeros_like(acc_ref)
    carry = (tuple(jnp.full((1, tq), -jnp.inf, F32) for _ in range(nit)),
             tuple(jnp.zeros((SUBLANES, tq), F32) for _ in range(nit)), colmax(cm8))
    carry = lax.fori_loop(0, nkt - 1, lambda c, cr: trip(c, cr, False), carry)
    _, ls, _ = trip(nkt - 1, carry, True)
    for it in range(nit):
        u, g = divmod(it, A_GROUP)
        l = jnp.sum(ls[it], axis=0, keepdims=True)
        o_ref[u * tq:(u + 1) * tq, g * A_HD:(g + 1) * A_HD] = (acc_ref[it] / l).T.astype(BF16)


def _attention(aq, ak, avt, bsz, ntile, q_off, nq, nk, nqs):
    tq = TOKEN_TILE
    tt = avt.shape[2]
    sub = min(ATTN_KV_TILE, nk) // tt
    tk = sub * tt
    nt = ntile * tq
    assert nt % nk == 0 and nk % tk == 0 and nq % nqs == 0
    nsteps = nq // nqs
    q_spec = lambda u: pl.BlockSpec((tq, A_GROUP * A_HD), lambda b, h, i: (b * ntile + q_off + i * nqs + u, h))
    return pl.pallas_call(
        functools.partial(_attn_kernel, nkt=nk // tk, sub=sub, nqs=nqs),
        grid=(bsz, A_KV_HEADS, nsteps),
        in_specs=[q_spec(u) for u in range(nqs)] + [
            pl.BlockSpec((nk, A_HD), lambda b, h, i: (b * (nt // nk), h)),
            pl.BlockSpec((nk // tt, A_HD, tt), lambda b, h, i: (b * (nt // nk), h, 0)),
        ],
        out_specs=pl.BlockSpec((nqs * tq, A_GROUP * A_HD), lambda b, h, i: (b * nsteps + i, h)),
        out_shape=jax.ShapeDtypeStruct((bsz * nq * tq, A_Q), BF16),
        scratch_shapes=[pltpu.VMEM((nqs * A_GROUP, A_HD, tq), F32), pltpu.VMEM((2, tk, tq), F32)],
        compiler_params=_params(("parallel", "parallel", "arbitrary")),
        name="gqa_attention",
    )(*([aq] * nqs), ak, avt)


def _mix_kernel(x_ref, mod_ref, hf_ref, hr_ref, o_ref, mg_ref, yac_ref, yal_ref, cu_ref, cup_ref, cun_ref, b_ref,
                cw_ref, sg_ref, wpm_ref, wpa_ref, wpc_ref, wo_ref, n2_ref, wup_ref, wdn_ref, fg_ref, out_ref,
                *, ntile, tile_off, final):
    pos = pl.program_id(1) + tile_off
    tm = x_ref.shape[0]
    hsum = hf_ref[...].astype(F32) + hr_ref[...].astype(F32)
    parts = []
    for hd in range(M_HEADS):
        hh = hsum[:, hd * M_DV:(hd + 1) * M_DV]
        parts.append(hh * lax.rsqrt(jnp.mean(hh * hh, axis=-1, keepdims=True) + EPS))
    hn = jnp.concatenate(parts, axis=-1) * mg_ref[...]
    ym = (hn * _sigmoid(o_ref[...].astype(F32))).astype(BF16)
    ya = jnp.where(pos == 0, yac_ref[...], yal_ref[...])
    has_prev = pos >= 2
    has_next = jnp.logical_and(pos >= 1, pos <= ntile - 2)
    cu = cu_ref[...].astype(F32)
    prev_row = jnp.where(has_prev, cup_ref[BF16_SUBLANES - 1:BF16_SUBLANES, :].astype(F32), 0.0)
    next_row = jnp.where(has_next, cun_ref[0:1, :].astype(F32), 0.0)
    ridx = lax.broadcasted_iota(jnp.int32, cu.shape, 0)
    up = jnp.where(ridx == 0, prev_row, pltpu.roll(cu, 1, 0))
    dn = jnp.where(ridx == tm - 1, next_row, pltpu.roll(cu, tm - 1, 0))
    conv = cw_ref[0:1, :] * up + cw_ref[1:2, :] * cu + cw_ref[2:3, :] * dn
    yc = (b_ref[...].astype(F32) * conv).astype(BF16)
    w = D_MODEL
    merged = (sg_ref[:, 0:w].astype(F32) * jnp.dot(ym, wpm_ref[...], preferred_element_type=F32)
              + sg_ref[:, w:2 * w].astype(F32) * jnp.dot(ya, wpa_ref[...], preferred_element_type=F32)
              + sg_ref[:, 2 * w:3 * w].astype(F32) * jnp.dot(yc, wpc_ref[...], preferred_element_type=F32))
    out = jnp.dot(merged.astype(BF16), wo_ref[...], preferred_element_type=F32)
    x1 = x_ref[...] + mod_ref[2:3, :] * out
    h = _norm_mod(x1, n2_ref[...], mod_ref[3:4, :], mod_ref[4:5, :]).astype(BF16)
    acc = jnp.zeros(x1.shape, F32)
    fc = D_FF // 4
    for c in range(4):
        a = jnp.dot(h, wup_ref[:, c * fc:(c + 1) * fc], preferred_element_type=F32)
        a = jnp.square(jnp.maximum(a, 0.0)).astype(BF16)
        acc = acc + jnp.dot(a, wdn_ref[c * fc:(c + 1) * fc, :], preferred_element_type=F32)
    y = x1 + mod_ref[5:6, :] * acc
    if final:
        ms = jnp.mean(y * y, axis=-1, keepdims=True)
        y = y * lax.rsqrt(ms + EPS) * fg_ref[...]
    out_ref[...] = y


def _mix(xc, modsel, h2, o, mgain, ya_ctx, ya_lat, cu, bg, conv_w, sg, wpm, wpa, wpc, wo, n2, wup, wdn, final_g,
         bsz, ntile, tile_off, final):
    t = xc.shape[0]
    tm = TOKEN_TILE
    hb = tm // BF16_SUBLANES
    nhb = t // BF16_SUBLANES
    nout = ntile - tile_off
    tile = lambda b, i: b * ntile + tile_off + i
    tok = lambda n: pl.BlockSpec((tm, n), lambda b, i: (tile(b, i), 0))
    return pl.pallas_call(
        functools.partial(_mix_kernel, ntile=ntile, tile_off=tile_off, final=final),
        grid=(bsz, nout),
        in_specs=[
            tok(D_MODEL),
            pl.BlockSpec((None, None, N_MOD, D_MODEL), lambda b, i: (b, jnp.minimum(i + tile_off, 1), 0, 0)),
            pl.BlockSpec((None, tm, M_V), lambda b, i: (0, tile(b, i), 0)),
            pl.BlockSpec((None, tm, M_V), lambda b, i: (1, tile(b, i), 0)),
            tok(M_V),
            _const_spec((1, M_V)),
            pl.BlockSpec((tm, A_Q), lambda b, i: (b, 0)),
            pl.BlockSpec((tm, A_Q), lambda b, i: (b * (ntile - 1) + jnp.maximum(i + tile_off - 1, 0), 0)),
            tok(C_WIDTH),
            pl.BlockSpec((BF16_SUBLANES, C_WIDTH), lambda b, i: (jnp.maximum(tile(b, i) * hb - 1, 0), 0)),
            pl.BlockSpec((BF16_SUBLANES, C_WIDTH), lambda b, i: (jnp.minimum((tile(b, i) + 1) * hb, nhb - 1), 0)),
            tok(C_WIDTH),
            _const_spec((3, C_WIDTH)),
            tok(3 * D_MODEL),
            _const_spec((M_V, D_MODEL)), _const_spec((A_Q, D_MODEL)), _const_spec((C_WIDTH, D_MODEL)),
            _const_spec((D_MODEL, D_MODEL)),
            _const_spec((1, D_MODEL)),
            _const_spec((D_MODEL, D_FF)),
            _const_spec((D_FF, D_MODEL)),
            _const_spec((1, D_MODEL)),
        ],
        out_specs=pl.BlockSpec((tm, D_MODEL), lambda b, i: (b * nout + i, 0)),
        out_shape=jax.ShapeDtypeStruct((bsz * nout * tm, D_MODEL), F32),
        compiler_params=_params(("parallel", "parallel")),
        name="mix_and_mlp",
    )(xc, modsel, h2, h2, o, mgain, ya_ctx, ya_lat, cu, cu, cu, bg, conv_w, sg, wpm, wpa, wpc, wo, n2, wup, wdn,
      final_g)


def _rope_perm():
    p = np.arange(A_HD)
    part, axis, f = p // 64, (p // 32) % 2, p % 32
    return axis * 64 + part * 32 + f


def _rope_tables(n_ctx, n_lat):
    rows = n_lat // GRID_W
    row_idx = jnp.repeat(jnp.arange(rows), GRID_W)
    col_idx = jnp.tile(jnp.arange(GRID_W), rows)
    n_freq = A_HD // 4
    inv_freq = ROPE_THETA ** (-jnp.arange(n_freq, dtype=F32) / n_freq)
    pos = jnp.stack([row_idx, col_idx], axis=-1).astype(F32)
    ang = pos[:, :, None] * inv_freq
    cos = jnp.cos(ang).reshape(n_lat, 2 * n_freq)
    sin = jnp.sin(ang).reshape(n_lat, 2 * n_freq)
    cos_t = jnp.concatenate([cos, cos], axis=-1)
    sin_t = jnp.concatenate([-sin, sin], axis=-1)
    cos_t = jnp.concatenate([jnp.ones((n_ctx, A_HD), F32), cos_t], axis=0)
    sin_t = jnp.concatenate([jnp.zeros((n_ctx, A_HD), F32), sin_t], axis=0)
    return cos_t, sin_t


def kernel(x, c, ctx, c_ctx, w_ada, b_ada, norm1, norm2, w_in, mlstm_gate_b, mlstm_norm, q_norm, k_norm,
           conv_w, w_proj_m, w_proj_a, w_proj_c, w_out, w_up, w_down, final_norm):
    bsz, n_lat, _ = x.shape
    n_ctx = ctx.shape[1]
    depth = w_in.shape[0]
    assert n_ctx == TOKEN_TILE and n_ctx == M_CHUNK and n_lat % TOKEN_TILE == 0 and bsz + 1 <= 8
    nt = n_ctx + n_lat
    ntile = nt // TOKEN_TILE
    nch = nt // M_CHUNK

    xc = jnp.concatenate([ctx, x], axis=1).reshape(bsz * nt, D_MODEL)
    c_rows = jnp.zeros((8, D_MODEL), F32).at[:bsz].set(c).at[bsz].set(c_ctx)
    mods = _modulation(c_rows, w_ada, b_ada).reshape(depth, 8, N_MOD, D_MODEL)
    cos_t, sin_t = _rope_tables(n_ctx, n_lat)
    perm = _rope_perm()
    head_perm = lambda n: np.concatenate([h * A_HD + perm for h in range(n)])
    fg = final_norm.reshape(1, D_MODEL)
    w_in_b = w_in.astype(BF16)
    wb = [w.astype(BF16) for w in (w_proj_m, w_proj_a, w_proj_c, w_out, w_up, w_down)]

    for layer in range(depth):
        need_ctx = layer < depth - 1
        tile_off = 0 if need_ctx else 1
        ml = mods[layer]
        modsel = jnp.stack([jnp.broadcast_to(ml[bsz], (bsz, N_MOD, D_MODEL)), ml[:bsz]], axis=1)
        wl = w_in_b[layer]
        o = COL_OFF
        w_m = jnp.concatenate([wl[:, o[1]:o[2]], wl[:, o[3]:o[4]]], axis=1)
        w_mt = jnp.concatenate([wl[:, o[0]:o[1]], wl[:, o[2]:o[3]]], axis=1).T
        w_g = wl[:, o[4]:o[5]]
        w_gp = jnp.zeros((D_MODEL, 2 * LANES), BF16)
        w_gp = w_gp.at[:, 0:2 * M_HEADS].set(w_g[:, 0:2 * M_HEADS])
        w_gp = w_gp.at[:, LANES:LANES + 2 * M_HEADS].set(w_g[:, 2 * M_HEADS:])
        w_gt = w_g.T
        gb = mlstm_gate_b[layer].reshape(M_GATES)
        gbp = jnp.zeros((1, 2 * LANES), F32)
        gbp = gbp.at[0, 0:2 * M_HEADS].set(gb[0:2 * M_HEADS]).at[0, LANES:LANES + 2 * M_HEADS].set(gb[2 * M_HEADS:])
        gbt = gb.reshape(M_GATES, 1)
        w_a = jnp.concatenate([wl[:, o[5]:o[6]][:, head_perm(A_HEADS)], wl[:, o[6]:o[7]][:, head_perm(A_KV_HEADS)]],
                              axis=1)
        w_vt = wl[:, o[7]:o[8]].T
        w_c = wl[:, o[8]:o[14]]
        n1 = norm1[layer].reshape(1, D_MODEL)
        n2 = norm2[layer].reshape(1, D_MODEL)

        mk, mqt, mvt, mo, gates, gates_t = _inproj_mlstm(xc, modsel, n1, w_m, w_mt, w_gp, w_gt, gbp, gbt, ntile)
        aq, ak, avt = _inproj_attn(xc, modsel, n1, w_a, w_vt, q_norm[layer][perm].reshape(1, A_HD),
                                   k_norm[layer][perm].reshape(1, A_HD), cos_t, sin_t, ntile)
        cu, bg, sg = _inproj_conv(xc, modsel, n1, w_c, ntile)

        h2 = _mlstm(mk, mqt, mvt, gates, gates_t, bsz, nch)
        ya_lat = _attention(aq, ak, avt, bsz, ntile, 1, ntile - 1, nt, ATTN_Q_TILES)
        ya_ctx = _attention(aq, ak, avt, bsz, ntile, 0, 1, n_ctx, 1) if need_ctx else ya_lat

        xc = _mix(xc, modsel, h2, mo, mlstm_norm[layer].reshape(1, M_V), ya_ctx, ya_lat, cu, bg, conv_w[layer], sg,
                  wb[0][layer], wb[1][layer], wb[2][layer], wb[3][layer], n2, wb[4][layer], wb[5][layer], fg,
                  bsz, ntile, tile_off, not need_ctx)

    return xc.reshape(bsz, n_lat, D_MODEL)
```
